```python
import functools
import jax, jax.numpy as jnp
from jax import lax
import numpy as np

D_MODEL = 1024
BATCH = 16
SEQ = 2048
DEPTH = 1
DEC_BATCH = 32
DEC_SEQ = 4
PAST_LEN = 16384
PAGE_SIZE = 128

N_HEADS = 8
HEAD_DIM = 64
ATT_W = N_HEADS * HEAD_DIM
MOBA_BLOCK = 256
MOBA_TOPK = 3
Q_CHUNK = 128
POOL_WINDOWS = (2, 4, 8, 16)
POOL_GROUPS = len(POOL_WINDOWS)
POOL_W = D_MODEL // 2
POOL_GW = POOL_W // POOL_GROUPS
POOL_CTX = max(POOL_WINDOWS) - 1
D_FF = ((8 * D_MODEL // 3 + 255) // 256) * 256
CONV_W = 3
N_MOD = 6
EPS = 1e-6
IN_SPLITS = (POOL_W, ATT_W, ATT_W, ATT_W, D_MODEL, D_MODEL)
ADA_INIT = 0.5
NOISE = 0.05

kernel_name = 'moba_pool_hybrid_decode_step'


def rms_norm(x, gain):
    xf = x.astype(jnp.float32)
    y = xf * lax.rsqrt(jnp.mean(xf * xf, axis=-1, keepdims=True) + EPS)
    return (y * gain.astype(jnp.float32)).astype(x.dtype)


def ada_modulation(c, w_ada, b_ada):
    mod = (jax.nn.silu(c) @ w_ada + b_ada).reshape(c.shape[0], N_MOD, D_MODEL)
    return [mod[:, None, i] for i in range(N_MOD)]


def pool_mixer(u, u_past, pos0, w_pool_group, pool_scale):
    n, s, _ = u.shape
    ext = jnp.concatenate([jnp.zeros((n, 1, POOL_W), u.dtype), u_past, u], axis=1).astype(jnp.float32)
    cs = jnp.cumsum(ext, axis=1)
    end = cs[:, POOL_CTX + 1:]
    pos = pos0 + jnp.arange(s)
    groups = []
    for g, w in enumerate(POOL_WINDOWS):
        sl = slice(g * POOL_GW, (g + 1) * POOL_GW)
        start = cs[:, POOL_CTX + 1 - w:POOL_CTX + 1 - w + s, sl]
        cnt = jnp.minimum(w, pos + 1).astype(jnp.float32)[None, :, None]
        groups.append((end[..., sl] - start) / cnt)
    pooled = jnp.concatenate(groups, axis=-1) - u.astype(jnp.float32)
    pg = pooled.astype(u.dtype).reshape(n, s, POOL_GROUPS, POOL_GW)
    y = jnp.einsum('nsgc,gcd->nsgd', pg, w_pool_group).reshape(n, s, POOL_W)
    return y * pool_scale


def causal_dwconv(a, a_past, w_conv, b_conv):
    s = a.shape[1]
    ext = jnp.concatenate([a_past, a], axis=1)
    y = b_conv
    for j in range(CONV_W):
        y = y + ext[:, j:j + s] * w_conv[j]
    return y


def moba_prompt(q, k, v):
    n, t = q.shape[:2]
    n_blk = -(-t // MOBA_BLOCK)
    t_pad = n_blk * MOBA_BLOCK
    scale = HEAD_DIM ** -0.5
    qh, kh, vh = (a.transpose(0, 2, 1, 3) for a in (q, k, v))
    pad = ((0, 0), (0, 0), (0, t_pad - t), (0, 0))
    kb = jnp.pad(kh, pad).reshape(n, N_HEADS, n_blk, MOBA_BLOCK, HEAD_DIM)
    vb = jnp.pad(vh, pad).reshape(n, N_HEADS, n_blk, MOBA_BLOCK, HEAD_DIM)
    k_mean = jnp.mean(kb.astype(jnp.float32), axis=3).astype(q.dtype)
    blk_score = jnp.einsum('nhtd,nhbd->nhtb', qh, k_mean).astype(jnp.float32)
    q_blk = jnp.arange(t) // MOBA_BLOCK
    past_blk = jnp.arange(n_blk)[None, :] < q_blk[:, None]
    blk_score = jnp.where(past_blk, blk_score, -jnp.inf)
    top = min(MOBA_TOPK, n_blk)
    _, sel = lax.top_k(blk_score, top)
    sel_ok = jnp.arange(top)[None, :] < q_blk[:, None]
    n_chunk = t // Q_CHUNK
    q_c = qh.reshape(n, N_HEADS, n_chunk, Q_CHUNK, HEAD_DIM).transpose(2, 0, 1, 3, 4)
    sel_c = sel.reshape(n, N_HEADS, n_chunk, Q_CHUNK, top).transpose(2, 0, 1, 3, 4)
    ok_c = sel_ok.reshape(n_chunk, Q_CHUNK, top)
    bi = jnp.arange(n)[:, None, None, None]
    hi = jnp.arange(N_HEADS)[None, :, None, None]

    def chunk(args):
        qc, selc, okc, ci = args
        q_pos = ci * Q_CHUNK + jnp.arange(Q_CHUNK)
        own = (ci * Q_CHUNK) // MOBA_BLOCK
        k_own = lax.dynamic_index_in_dim(kb, own, axis=2, keepdims=False)
        v_own = lax.dynamic_index_in_dim(vb, own, axis=2, keepdims=False)
        k_pos = own * MOBA_BLOCK + jnp.arange(MOBA_BLOCK)
        k_sel = kb[bi, hi, selc].reshape(n, N_HEADS, Q_CHUNK, top * MOBA_BLOCK, HEAD_DIM)
        v_sel = vb[bi, hi, selc].reshape(n, N_HEADS, Q_CHUNK, top * MOBA_BLOCK, HEAD_DIM)
        ls = jnp.einsum('nhqd,nhqkd->nhqk', qc, k_sel).astype(jnp.float32) * scale
        ls = jnp.where(jnp.repeat(okc, MOBA_BLOCK, axis=-1)[None, None], ls, -jnp.inf)
        lo = jnp.einsum('nhqd,nhjd->nhqj', qc, k_own).astype(jnp.float32) * scale
        lo = jnp.where((k_pos[None, :] <= q_pos[:, None])[None, None], lo, -jnp.inf)
        p = jax.nn.softmax(jnp.concatenate([ls, lo], axis=-1), axis=-1).astype(vb.dtype)
        p_sel, p_own = p[..., :top * MOBA_BLOCK], p[..., top * MOBA_BLOCK:]
        return (jnp.einsum('nhqk,nhqkd->nhqd', p_sel, v_sel)
                + jnp.einsum('nhqj,nhjd->nhqd', p_own, v_own))

    out = lax.map(chunk, (q_c, sel_c, ok_c, jnp.arange(n_chunk)))
    return out.transpose(1, 0, 3, 2, 4).reshape(n, t, N_HEADS, HEAD_DIM)


def moba_sample(q, k, v, cache_k, cache_v, page_table, layer):
    n, s = q.shape[:2]
    past = page_table.shape[1] * PAGE_SIZE
    ppb = MOBA_BLOCK // PAGE_SIZE
    n_blk = past // MOBA_BLOCK
    own_past = past - n_blk * MOBA_BLOCK
    scale = HEAD_DIM ** -0.5
    qh, kh, vh = (a.transpose(0, 2, 1, 3) for a in (q, k, v))
    logits, values = [], []
    if n_blk > 0:
        k_past = cache_k[layer, page_table[:, :n_blk * ppb]]
        k_mean = jnp.mean(k_past.reshape(n, n_blk, MOBA_BLOCK, N_HEADS, HEAD_DIM).astype(jnp.float32),
                          axis=2).astype(q.dtype)
        blk_score = jnp.einsum('nhsd,nbhd->nhsb', qh, k_mean).astype(jnp.float32)
        top = min(MOBA_TOPK, n_blk)
        _, sel = lax.top_k(blk_score, top)
        logical = sel[..., None] * ppb + jnp.arange(ppb)
        phys = page_table[jnp.arange(n)[:, None, None, None, None], logical]
        hi = jnp.arange(N_HEADS)[None, :, None, None, None]
        k_sel = cache_k[layer, phys, :, hi].reshape(n, N_HEADS, s, top * MOBA_BLOCK, HEAD_DIM)
        v_sel = cache_v[layer, phys, :, hi].reshape(n, N_HEADS, s, top * MOBA_BLOCK, HEAD_DIM)
        logits.append(jnp.einsum('nhsd,nhskd->nhsk', qh, k_sel).astype(jnp.float32) * scale)
        values.append(v_sel)
    if own_past > 0:
        first = n_blk * ppb
        k_op = cache_k[layer, page_table[:, first:]].reshape(n, own_past, N_HEADS, HEAD_DIM).transpose(0, 2, 1, 3)
        v_op = cache_v[layer, page_table[:, first:]].reshape(n, own_past, N_HEADS, HEAD_DIM).transpose(0, 2, 1, 3)
        k_own = jnp.concatenate([k_op, kh], axis=2)
        v_own = jnp.concatenate([v_op, vh], axis=2)
    else:
        k_own, v_own = kh, vh
    j = jnp.arange(own_past + s)
    own_mask = (j[None, :] < own_past) | ((j[None, :] - own_past) <= jnp.arange(s)[:, None])
    lo = jnp.einsum('nhsd,nhjd->nhsj', qh, k_own).astype(jnp.float32) * scale
    logits.append(jnp.where(own_mask[None, None], lo, -jnp.inf))
    p = jax.nn.softmax(jnp.concatenate(logits, axis=-1), axis=-1).astype(v.dtype)
    n_sel = p.shape[-1] - (own_past + s)
    o = jnp.einsum('nhsj,nhjd->nhsd', p[..., n_sel:], v_own)
    if n_blk > 0:
        o = o + jnp.einsum('nhsk,nhskd->nhsd', p[..., :n_sel], values[0])
    return o.transpose(0, 2, 1, 3)


def decoder_layer(x, c, pool_past, conv_past, pos0, attend,
                  w_ada, b_ada, g_norm_mix, w_in, g_q, g_k, w_pool_group, pool_scale,
                  w_branch_pool, w_branch_attn, w_out, g_norm_ffn, w_up, w_conv, b_conv, w_down):
    n, s, _ = x.shape
    sh_m, sc_m, gt_m, sh_f, sc_f, gt_f = ada_modulation(c, w_ada, b_ada)
    h = rms_norm(x, g_norm_mix) * (1 + sc_m) + sh_m
    cuts = np.cumsum(IN_SPLITS)[:-1].tolist()
    u, q, k, v, g_a, g_b = jnp.split(h @ w_in, cuts, axis=-1)
    a = pool_mixer(u, pool_past, pos0, w_pool_group, pool_scale) @ w_branch_pool
    q = rms_norm(q.reshape(n, s, N_HEADS, HEAD_DIM), g_q)
    k = rms_norm(k.reshape(n, s, N_HEADS, HEAD_DIM), g_k)
    v = v.reshape(n, s, N_HEADS, HEAD_DIM)
    o = attend(q, k, v).reshape(n, s, ATT_W) @ w_branch_attn
    merged = jax.nn.sigmoid(g_a) * a + jax.nn.sigmoid(g_b) * o
    x = x + gt_m * (merged @ w_out)
    h2 = rms_norm(x, g_norm_ffn) * (1 + sc_f) + sh_f
    f_gate, f_val = jnp.split(h2 @ w_up, 2, axis=-1)
    f_conv = causal_dwconv(f_gate, conv_past, w_conv, b_conv)
    x = x + gt_f * ((jax.nn.silu(f_conv) * f_val) @ w_down)
    new_pool = jnp.concatenate([pool_past, u], axis=1)[:, -POOL_CTX:]
    new_conv = jnp.concatenate([conv_past, f_gate], axis=1)[:, -(CONV_W - 1):]
    return x, k, v, new_pool, new_conv


def setup_inputs(seed: int = 0) -> dict:
    key = jax.random.key(seed)
    ks = jax.random.split(key, 32)
    nrm = jax.random.normal
    n_pages = PAST_LEN // PAGE_SIZE
    n_used = DEC_BATCH * n_pages
    n_phys = n_used + max(1, n_used // 4)
    page_table = jax.random.permutation(ks[0], n_phys)[:n_used].reshape(DEC_BATCH, n_pages).astype(jnp.int32)
    d_in = sum(IN_SPLITS)
    return {
        'x_prompt': nrm(ks[1], (BATCH, SEQ, D_MODEL), jnp.float32),
        'x_sample': nrm(ks[2], (DEC_BATCH, DEC_SEQ, D_MODEL), jnp.float32),
        'cache_k': nrm(ks[3], (DEPTH, n_phys, PAGE_SIZE, N_HEADS, HEAD_DIM), jnp.float32),
        'cache_v': nrm(ks[4], (DEPTH, n_phys, PAGE_SIZE, N_HEADS, HEAD_DIM), jnp.float32),
        'state_pool': nrm(ks[5], (DEPTH, DEC_BATCH, POOL_CTX, POOL_W), jnp.float32),
        'state_ffn_conv': nrm(ks[6], (DEPTH, DEC_BATCH, CONV_W - 1, D_FF), jnp.float32),
        'page_table': page_table,
        'c_prompt': nrm(ks[7], (BATCH, D_MODEL), jnp.float32),
        'c_sample': nrm(ks[8], (DEC_BATCH, D_MODEL), jnp.float32),
        'w_ada': nrm(ks[9], (DEPTH, D_MODEL, N_MOD * D_MODEL), jnp.float32) * (ADA_INIT * D_MODEL ** -0.5),
        'b_ada': nrm(ks[10], (DEPTH, N_MOD * D_MODEL), jnp.float32) * 0.02,
        'g_norm_mix': 1.0 + NOISE * nrm(ks[11], (DEPTH, D_MODEL), jnp.float32),
        'w_in': nrm(ks[12], (DEPTH, D_MODEL, d_in), jnp.float32) * D_MODEL ** -0.5,
        'g_q': 1.0 + NOISE * nrm(ks[13], (DEPTH, N_HEADS, HEAD_DIM), jnp.float32),
        'g_k': 1.0 + NOISE * nrm(ks[14], (DEPTH, N_HEADS, HEAD_DIM), jnp.float32),
        'w_pool_group': nrm(ks[15], (DEPTH, POOL_GROUPS, POOL_GW, POOL_GW), jnp.float32) * POOL_GW ** -0.5,
        'pool_scale': 1.0 + NOISE * nrm(ks[16], (DEPTH, POOL_W), jnp.float32),
        'w_branch_pool': nrm(ks[17], (DEPTH, POOL_W, D_MODEL), jnp.float32) * POOL_W ** -0.5,
        'w_branch_attn': nrm(ks[18], (DEPTH, ATT_W, D_MODEL), jnp.float32) * ATT_W ** -0.5,
        'w_out': nrm(ks[19], (DEPTH, D_MODEL, D_MODEL), jnp.float32) * D_MODEL ** -0.5,
        'g_norm_ffn': 1.0 + NOISE * nrm(ks[20], (DEPTH, D_MODEL), jnp.float32),
        'w_up': nrm(ks[21], (DEPTH, D_MODEL, 2 * D_FF), jnp.float32) * D_MODEL ** -0.5,
        'w_conv': nrm(ks[22], (DEPTH, CONV_W, D_FF), jnp.float32) * CONV_W ** -0.5,
        'b_conv': nrm(ks[23], (DEPTH, D_FF), jnp.float32) * 0.02,
        'w_down': nrm(ks[24], (DEPTH, D_FF, D_MODEL), jnp.float32) * D_FF ** -0.5,
    }


def reference(x_prompt, x_sample, cache_k, cache_v, state_pool, state_ffn_conv, page_table,
              c_prompt, c_sample, w_ada, b_ada, g_norm_mix, w_in, g_q, g_k, w_pool_group, pool_scale,
              w_branch_pool, w_branch_attn, w_out, g_norm_ffn, w_up, w_conv, b_conv, w_down):
    y_p, y_s = x_prompt, x_sample
    k_p, v_p, pool_p, conv_p = [], [], [], []
    k_s, v_s, pool_s, conv_s = [], [], [], []
    nb = x_prompt.shape[0]
    zero_pool = jnp.zeros((nb, POOL_CTX, POOL_W), x_prompt.dtype)
    zero_conv = jnp.zeros((nb, CONV_W - 1, D_FF), x_prompt.dtype)
    for l in range(DEPTH):
        lw = (w_ada[l], b_ada[l], g_norm_mix[l], w_in[l], g_q[l], g_k[l], w_pool_group[l], pool_scale[l],
              w_branch_pool[l], w_branch_attn[l], w_out[l], g_norm_ffn[l], w_up[l], w_conv[l], b_conv[l], w_down[l])
        y_p, kl, vl, pl, cl = decoder_layer(y_p, c_prompt, zero_pool, zero_conv, 0, moba_prompt, *lw)
        k_p.append(kl); v_p.append(vl); pool_p.append(pl); conv_p.append(cl)
        attend_s = functools.partial(moba_sample, cache_k=cache_k, cache_v=cache_v,
                                     page_table=page_table, layer=l)
        y_s, kl, vl, pl, cl = decoder_layer(y_s, c_sample, state_pool[l], state_ffn_conv[l], PAST_LEN,
                                            attend_s, *lw)
        k_s.append(kl); v_s.append(vl); pool_s.append(pl); conv_s.append(cl)
    return (y_p, y_s, jnp.stack(k_p), jnp.stack(v_p), jnp.stack(pool_p), jnp.stack(conv_p),
            jnp.stack(k_s), jnp.stack(v_s), jnp.stack(pool_s), jnp.stack(conv_s))
```

```python
import functools

import jax
import jax.numpy as jnp
from jax import lax
from jax.experimental import pallas as pl
from jax.experimental.pallas import tpu as pltpu

F32 = jnp.float32
BF16 = jnp.bfloat16

MOBA_BLOCK = 256
MOBA_TOPK = 3
POOL_WINDOWS = (2, 4, 8, 16)
POOL_CTX = max(POOL_WINDOWS) - 1
CONV_W = 3
N_MOD = 6
EPS = 1e-6
NEG = -1e30

LANES = 128
SUBLANES = 8
MXU_DIM = 256
VMEM_LIMIT = 56 * 1024 * 1024


def _const_spec(shape):
    nd = len(shape)
    return pl.BlockSpec(shape, lambda *_: (0,) * nd, pipeline_mode=pl.Buffered(1))


def _dot(a, b):
    return jnp.dot(a, b, preferred_element_type=F32)


def _dot_nt(a, b):
    return lax.dot_general(a, b, (((1,), (1,)), ((), ())), preferred_element_type=F32)


def _sigmoid(x):
    return 1.0 / (1.0 + jnp.exp(-x))


def _ada_body(c_ref, w_ref, b_ref, o_ref):
    c = c_ref[...]
    a = (c * _sigmoid(c)).astype(BF16)
    o_ref[...] = _dot(a, w_ref[...]) + b_ref[...]


def _ada_call(c, w_ada, b_ada):
    n, d = c.shape
    dm = w_ada.shape[1]
    bn = d
    return pl.pallas_call(
        _ada_body,
        grid=(dm // bn,),
        in_specs=[pl.BlockSpec((n, d), lambda j: (0, 0)),
                  pl.BlockSpec((d, bn), lambda j: (0, j)),
                  pl.BlockSpec((1, bn), lambda j: (0, j))],
        out_specs=pl.BlockSpec((n, bn), lambda j: (0, j)),
        out_shape=jax.ShapeDtypeStruct((n, dm), F32),
        name="ada_mod",
    )(c, w_ada, b_ada.reshape(1, dm))


def _inproj_body(x_ref, mod_ref, gmix_ref, wu_ref, wq_ref, wkt_ref, wvt_ref, wg_ref, gq_ref, gk_ref, bd_ref,
                 wpg_ref, psc_ref, wbp_ref, st_ref,
                 kt_ref, vt_ref, q_ref, ag_ref, gb_ref, pool_ref, ubuf,
                 *, tm, rows_per_tok, halo, pos0, nt, n_heads, hd, scale):
    t = pl.program_id(1)
    d_att = n_heads * hd
    d_model = x_ref.shape[-1]
    x = x_ref[0]
    sh_m = mod_ref[0, 0]
    sc_m = mod_ref[0, 1]
    ms = jnp.mean(x * x, axis=-1, keepdims=True)
    h = x * lax.rsqrt(ms + EPS) * gmix_ref[...]
    h = (h * (1.0 + sc_m) + sh_m).astype(BF16)

    kt = _dot_nt(wkt_ref[...], h)
    vt_ref[0] = _dot_nt(wvt_ref[...], h)
    for hh in range(n_heads):
        rs = slice(hh * hd, (hh + 1) * hd)
        kh = kt[rs]
        msk = jnp.sum(kh * kh, axis=0, keepdims=True) * (1.0 / hd)
        kt_ref[0, rs, :] = kh * lax.rsqrt(msk + EPS) * gk_ref[rs]

    q = _dot(h, wq_ref[...])
    qq = q * q
    hi = qq.astype(BF16)
    lo = (qq - hi.astype(F32)).astype(BF16)
    parts = []
    for c in range(d_att // MXU_DIM):
        cs = slice(c * MXU_DIM, (c + 1) * MXU_DIM)
        parts.append(_dot(hi[:, cs], bd_ref[...]) + _dot(lo[:, cs], bd_ref[...]))
    msq = jnp.concatenate(parts, axis=1)
    q_ref[0] = (q * lax.rsqrt(msq + EPS) * gq_ref[...] * scale).astype(BF16)

    u = _dot(h, wu_ref[...])

    @pl.when(t == 0)
    def _():
        ubuf[0:halo, :] = st_ref[0]

    @pl.when(t > 0)
    def _():
        ubuf[0:halo, :] = ubuf[tm:tm + halo, :]

    ubuf[halo:halo + tm, :] = u
    row = lax.broadcasted_iota(jnp.int32, (tm, 1), 0) + t * tm
    gw = LANES
    ys = []
    for g, w in enumerate(POOL_WINDOWS):
        cs = slice(g * gw, (g + 1) * gw)
        ug = u[:, cs]
        acc = ug
        cnt = 1.0
        for j in range(1, w):
            off = halo - j * rows_per_tok
            acc = acc + ubuf[off:off + tm, cs]
            first_row = (j - pos0) * rows_per_tok
            if first_row <= 0:
                cnt = cnt + 1.0
            else:
                cnt = cnt + (row >= first_row).astype(F32)
        pooled = (acc / cnt - ug).astype(BF16)
        ys.append(_dot(pooled, wpg_ref[g]))
    y = jnp.concatenate(ys, axis=1) * psc_ref[...]
    a = _dot(y.astype(BF16), wbp_ref[...])

    gates = _dot(h, wg_ref[...])
    ag_ref[0] = (_sigmoid(gates[:, :d_model]) * a).astype(BF16)
    gb_ref[0] = _sigmoid(gates[:, d_model:]).astype(BF16)

    @pl.when(t == nt - 1)
    def _():
        keep = POOL_CTX * rows_per_tok
        pool_ref[0] = ubuf[halo + tm - keep:halo + tm, :]


def _inproj_call(x, mod, state, wts, *, tm, rows_per_tok, halo, pos0, n_heads, hd):
    n, s, d = x.shape
    nt = s // tm
    d_att = n_heads * hd
    pw = wts["w_u"].shape[1]
    keep = POOL_CTX * rows_per_tok
    body = functools.partial(_inproj_body, tm=tm, rows_per_tok=rows_per_tok, halo=halo, pos0=pos0, nt=nt,
                             n_heads=n_heads, hd=hd, scale=hd ** -0.5)
    rm = mod.shape[2]
    in_specs = [
        pl.BlockSpec((1, tm, d), lambda i, t: (i, t, 0)),
        pl.BlockSpec((1, N_MOD, rm, d), lambda i, t: (i, 0, 0, 0)),
        _const_spec((1, d)),
        _const_spec(wts["w_u"].shape), _const_spec(wts["w_q"].shape), _const_spec(wts["w_kt"].shape),
        _const_spec(wts["w_vt"].shape), _const_spec(wts["w_g"].shape),
        _const_spec((1, d_att)), _const_spec((d_att, 1)), _const_spec((MXU_DIM, MXU_DIM)),
        _const_spec(wts["w_pg"].shape), _const_spec((1, pw)), _const_spec(wts["w_bp"].shape),
        pl.BlockSpec((1, halo, pw), lambda i, t: (i, 0, 0)),
    ]
    out_specs = [
        pl.BlockSpec((1, d_att, tm), lambda i, t: (i, 0, t)),
        pl.BlockSpec((1, d_att, tm), lambda i, t: (i, 0, t)),
        pl.BlockSpec((1, tm, d_att), lambda i, t: (i, t, 0)),
        pl.BlockSpec((1, tm, d), lambda i, t: (i, t, 0)),
        pl.BlockSpec((1, tm, d), lambda i, t: (i, t, 0)),
        pl.BlockSpec((1, keep, pw), lambda i, t: (i, 0, 0)),
    ]
    out_shape = [
        jax.ShapeDtypeStruct((n, d_att, s), F32),
        jax.ShapeDtypeStruct((n, d_att, s), F32),
        jax.ShapeDtypeStruct((n, s, d_att), BF16),
        jax.ShapeDtypeStruct((n, s, d), BF16),
        jax.ShapeDtypeStruct((n, s, d), BF16),
        jax.ShapeDtypeStruct((n, keep, pw), F32),
    ]
    return pl.pallas_call(
        body, grid=(n, nt), in_specs=in_specs, out_specs=out_specs, out_shape=out_shape,
        scratch_shapes=[pltpu.VMEM((halo + tm, pw), F32)],
        compiler_params=pltpu.CompilerParams(dimension_semantics=("arbitrary", "arbitrary"),
                                             vmem_limit_bytes=VMEM_LIMIT),
        name="mixer_inproj",
    )(x, mod, wts["g_mix"], wts["w_u"], wts["w_q"], wts["w_kt"], wts["w_vt"], wts["w_g"], wts["g_q"], wts["g_k"],
      wts["bd"], wts["w_pg"], wts["p_scale"], wts["w_bp"], state)


def _split3(x):
    a = x.astype(BF16)
    r = x - a.astype(F32)
    b = r.astype(BF16)
    c = (r - b.astype(F32)).astype(BF16)
    return a, b, c


def _attn_prompt_body(q_ref, kt_ref, vt_ref, avg_ref, o_ref, qa_ref, ka_ref, *, seq, nb, hd):
    blk = MOBA_BLOCK
    nbp = -(-nb // SUBLANES) * SUBLANES
    q2 = q_ref[0]
    kt2 = kt_ref[0]
    vt2 = vt_ref[0].astype(BF16)
    k1, k2, k3 = _split3(kt2)
    avg = avg_ref[...]
    kmt = _dot(k1, avg) + _dot(k2, avg) + _dot(k3, avg)
    lane_q = lax.broadcasted_iota(jnp.int32, (seq, 2 * hd), 1)
    rowh = lax.broadcasted_iota(jnp.int32, (2 * hd, LANES), 0)
    tpos = lax.broadcasted_iota(jnp.int32, (nbp, seq), 1)
    bidx = lax.broadcasted_iota(jnp.int32, (nbp, seq), 0)
    qblk = jnp.zeros((nbp, seq), jnp.int32)
    for b in range(1, nb):
        qblk = qblk + (tpos >= b * blk).astype(jnp.int32)
    onehot_t = (bidx == qblk).astype(F32)
    outs = []
    for hh in range(2):
        in_head = (rowh >= hh * hd) & (rowh < (hh + 1) * hd)
        kmh = jnp.where(in_head, kmt, 0.0).astype(BF16)
        sc = _dot(q2, kmh)
        sct = jnp.transpose(sc)[0:nbp]
        rank = jnp.zeros((nbp, seq), F32)
        for bp in range(nb):
            rowv = sct[bp:bp + 1, :]
            beats = (rowv > sct) | ((rowv == sct) & (bp < bidx))
            beats = beats & (bp < qblk)
            rank = rank + beats.astype(F32)
        keep = ((bidx < qblk) & (rank < MOBA_TOPK)) | (bidx == qblk)
        pen_t = jnp.where(keep, 0.0, NEG)
        zrows = lambda r: jnp.zeros((r, seq), F32)
        if hh == 0:
            pen_pad = jnp.concatenate([zrows(hd), pen_t, zrows(LANES - hd - nbp)], axis=0)
            ka = jnp.concatenate([kt2[0:hd], onehot_t, zrows(LANES - hd - nbp)], axis=0)
        else:
            pen_pad = jnp.concatenate([pen_t, zrows(LANES - nbp)], axis=0)
            ka = jnp.concatenate([onehot_t, zrows(hd - nbp), kt2[hd:2 * hd]], axis=0)
        pen_a = jnp.transpose(pen_pad).astype(BF16)
        own_lanes = (lane_q >= hh * hd) & (lane_q < (hh + 1) * hd)
        qa_ref[...] = jnp.where(own_lanes, q2, pen_a)
        ka_ref[...] = ka.astype(BF16)
        o_blocks = []
        for i in range(nb):
            ln = (i + 1) * blk
            s = _dot(qa_ref[i * blk:(i + 1) * blk, :], ka_ref[:, 0:ln])
            rr = lax.broadcasted_iota(jnp.int32, (blk, blk), 0)
            cc = lax.broadcasted_iota(jnp.int32, (blk, blk), 1)
            pieces = [s[:, j * blk:(j + 1) * blk] for j in range(i)]
            pieces.append(jnp.where(cc <= rr, s[:, i * blk:ln], NEG))
            mx = pieces[0]
            for pc in pieces[1:]:
                mx = jnp.maximum(mx, pc)
            m = jnp.max(mx, axis=-1, keepdims=True)
            ps = [jnp.exp(pc - m) for pc in pieces]
            tot = ps[0]
            for pc in ps[1:]:
                tot = tot + pc
            l = jnp.sum(tot, axis=-1, keepdims=True)
            pcat = jnp.concatenate([pc.astype(BF16) for pc in ps], axis=1) if i > 0 else ps[0].astype(BF16)
            o = _dot_nt(pcat, vt2[:, 0:ln])
            o_blocks.append(o / l)
        outs.append(jnp.concatenate(o_blocks, axis=0))
    o_ref[0] = jnp.where(lane_q < hd, outs[0], outs[1]).astype(BF16)


def _attn_prompt_call(q, kt, vt, *, n_heads, hd):
    n, s, d_att = q.shape
    nb = s // MOBA_BLOCK
    tpos = jnp.arange(s)[:, None] // MOBA_BLOCK
    avg = jnp.where(tpos == jnp.arange(LANES)[None, :], 1.0 / MOBA_BLOCK, 0.0).astype(BF16)
    body = functools.partial(_attn_prompt_body, seq=s, nb=nb, hd=hd)
    return pl.pallas_call(
        body, grid=(n, n_heads // 2),
        in_specs=[pl.BlockSpec((1, s, 2 * hd), lambda i, p: (i, 0, p)),
                  pl.BlockSpec((1, 2 * hd, s), lambda i, p: (i, p, 0)),
                  pl.BlockSpec((1, 2 * hd, s), lambda i, p: (i, p, 0)),
                  _const_spec((s, LANES))],
        out_specs=pl.BlockSpec((1, s, 2 * hd), lambda i, p: (i, 0, p)),
        out_shape=jax.ShapeDtypeStruct((n, s, d_att), BF16),
        scratch_shapes=[pltpu.VMEM((s, LANES), BF16), pltpu.VMEM((LANES, s), BF16)],
        compiler_params=pltpu.CompilerParams(dimension_semantics=("arbitrary", "arbitrary"),
                                             vmem_limit_bytes=VMEM_LIMIT),
        name="moba_prompt",
    )(q, kt, vt, avg)


def _post_body(x_ref, o_ref, ag_ref, gb_ref, mod_ref, wba_ref, wout_ref, gffn_ref, wup_ref, wconv_ref, bconv_ref,
               wdown_ref, cst_ref, y_ref, conv_ref, gbuf, *, tm, rows_per_tok, halo, nt, fc, d_ff):
    t = pl.program_id(1)
    x = x_ref[0]
    gt_m = mod_ref[0, 2]
    sh_f = mod_ref[0, 3]
    sc_f = mod_ref[0, 4]
    gt_f = mod_ref[0, 5]
    oa = _dot(o_ref[0], wba_ref[...])
    merged = ag_ref[0].astype(F32) + gb_ref[0].astype(F32) * oa
    x1 = x + gt_m * _dot(merged.astype(BF16), wout_ref[...])
    ms = jnp.mean(x1 * x1, axis=-1, keepdims=True)
    h2 = x1 * lax.rsqrt(ms + EPS) * gffn_ref[...]
    h2 = (h2 * (1.0 + sc_f) + sh_f).astype(BF16)

    @pl.when(t == 0)
    def _():
        gbuf[0:halo, :] = cst_ref[0]

    @pl.when(t > 0)
    def _():
        gbuf[0:halo, :] = gbuf[tm:tm + halo, :]

    acc = jnp.zeros_like(x)
    r = rows_per_tok
    for c in range(d_ff // fc):
        cs = slice(c * fc, (c + 1) * fc)
        fg = _dot(h2, wup_ref[:, cs])
        fv = _dot(h2, wup_ref[:, d_ff + c * fc:d_ff + (c + 1) * fc])
        gbuf[halo:halo + tm, cs] = fg
        conv = bconv_ref[:, cs]
        conv = conv + gbuf[halo - 2 * r:halo - 2 * r + tm, cs] * wconv_ref[0:1, cs]
        conv = conv + gbuf[halo - r:halo - r + tm, cs] * wconv_ref[1:2, cs]
        conv = conv + fg * wconv_ref[2:3, cs]
        act = (conv * _sigmoid(conv) * fv).astype(BF16)
        acc = acc + _dot(act, wdown_ref[cs, :])
    y_ref[0] = x1 + gt_f * acc

    @pl.when(t == nt - 1)
    def _():
        keep = (CONV_W - 1) * r
        conv_ref[0] = gbuf[halo + tm - keep:halo + tm, :]


def _post_call(x, o, ag, gb, mod, cstate, wts, *, tm, rows_per_tok, halo, fc):
    n, s, d = x.shape
    nt = s // tm
    d_att = o.shape[-1]
    d_ff = wts["w_down"].shape[0]
    keep = (CONV_W - 1) * rows_per_tok
    rm = mod.shape[2]
    body = functools.partial(_post_body, tm=tm, rows_per_tok=rows_per_tok, halo=halo, nt=nt, fc=fc, d_ff=d_ff)
    row_spec = lambda w: pl.BlockSpec((1, tm, w), lambda i, t: (i, t, 0))
    in_specs = [
        row_spec(d), row_spec(d_att), row_spec(d), row_spec(d),
        pl.BlockSpec((1, N_MOD, rm, d), lambda i, t: (i, 0, 0, 0)),
        _const_spec(wts["w_ba"].shape), _const_spec(wts["w_out"].shape), _const_spec((1, d)),
        _const_spec(wts["w_up"].shape), _const_spec((CONV_W, d_ff)), _const_spec((1, d_ff)),
        _const_spec(wts["w_down"].shape),
        pl.BlockSpec((1, halo, d_ff), lambda i, t: (i, 0, 0)),
    ]
    out_specs = [row_spec(d), pl.BlockSpec((1, keep, d_ff), lambda i, t: (i, 0, 0))]
    out_shape = [jax.ShapeDtypeStruct((n, s, d), F32), jax.ShapeDtypeStruct((n, keep, d_ff), F32)]
    return pl.pallas_call(
        body, grid=(n, nt), in_specs=in_specs, out_specs=out_specs, out_shape=out_shape,
        scratch_shapes=[pltpu.VMEM((halo + tm, d_ff), F32)],
        compiler_params=pltpu.CompilerParams(dimension_semantics=("arbitrary", "arbitrary"),
                                             vmem_limit_bytes=VMEM_LIMIT),
        name="merge_ffn",
    )(x, o, ag, gb, mod, wts["w_ba"], wts["w_out"], wts["g_ffn"], wts["w_up"], wts["w_conv"], wts["b_conv"],
      wts["w_down"], cstate)


_KM_SLOTS = 16


def _kmean_body(pt_ref, ck_ref, km_ref, buf, sem, *, n_pages, ppb):
    n = pl.program_id(0)
    rows = km_ref.shape[1]
    blocks_per_group = _KM_SLOTS // ppb
    n_groups = n_pages // _KM_SLOTS

    def copy(p, slot):
        return pltpu.make_async_copy(ck_ref.at[pt_ref[n, p]], buf.at[slot], sem.at[slot])

    for p in range(_KM_SLOTS):
        copy(p, p).start()
    km_ref[0] = jnp.zeros((rows, LANES), F32)
    lane = lax.broadcasted_iota(jnp.int32, (rows, LANES), 1)

    def group(g, carry):
        for bb in range(blocks_per_group):
            acc = None
            for pp in range(ppb):
                slot = bb * ppb + pp
                p = g * _KM_SLOTS + slot
                copy(p, slot).wait()
                page = buf[slot]
                acc = page if acc is None else acc + page

                @pl.when(p + _KM_SLOTS < n_pages)
                def _():
                    copy(p + _KM_SLOTS, slot).start()

            col = jnp.sum(acc, axis=-1, keepdims=True) * (1.0 / MOBA_BLOCK)
            b = g * blocks_per_group + bb
            km_ref[0] = jnp.where(lane == b, col, km_ref[0])
        return carry

    lax.fori_loop(0, n_groups, group, 0)


def _kmean_call(page_table, ck_t):
    n, n_pages = page_table.shape
    _, rows, page = ck_t.shape
    ppb = MOBA_BLOCK // page
    body = functools.partial(_kmean_body, n_pages=n_pages, ppb=ppb)
    grid_spec = pltpu.PrefetchScalarGridSpec(
        num_scalar_prefetch=1, grid=(n,),
        in_specs=[pl.BlockSpec(memory_space=pl.ANY)],
        out_specs=pl.BlockSpec((1, rows, LANES), lambda i, pt: (i, 0, 0)),
        scratch_shapes=[pltpu.VMEM((_KM_SLOTS, rows, page), F32), pltpu.SemaphoreType.DMA((_KM_SLOTS,))],
    )
    return pl.pallas_call(
        body, grid_spec=grid_spec,
        out_shape=jax.ShapeDtypeStruct((n, rows, LANES), F32),
        compiler_params=pltpu.CompilerParams(dimension_semantics=("arbitrary",), vmem_limit_bytes=VMEM_LIMIT),
        name="cache_block_means",
    )(page_table, ck_t)


def _topk_body(q_ref, km_ref, sel_ref, *, n_heads, hd, n_blk, top):
    n_seq = q_ref.shape[0]
    rows = n_heads * SUBLANES
    lane_q = lax.broadcasted_iota(jnp.int32, (SUBLANES, n_heads * hd), 1)
    lane = lax.broadcasted_iota(jnp.int32, (rows, LANES), 1).astype(F32)

    def per_seq(i, carry):
        q8 = q_ref[i]
        qbd = jnp.concatenate(
            [jnp.where((lane_q >= hh * hd) & (lane_q < (hh + 1) * hd), q8, 0.0) for hh in range(n_heads)], axis=0)
        sc = _dot(qbd.astype(BF16), km_ref[i].astype(BF16))
        sc = jnp.where(lane < n_blk, sc, -jnp.inf)
        out = jnp.zeros((rows, LANES), F32)
        for r in range(top):
            m = jnp.max(sc, axis=-1, keepdims=True)
            idx = jnp.min(jnp.where(sc == m, lane, float(LANES)), axis=-1, keepdims=True)
            out = jnp.where(lane == r, idx, out)
            sc = jnp.where(lane == idx, -jnp.inf, sc)
        sel_ref[i] = out.astype(jnp.int32)
        return carry

    lax.fori_loop(0, n_seq, per_seq, 0)


def _topk_call(q8, kmt, *, n_heads, hd, n_blk, top):
    n = q8.shape[0]
    rows = n_heads * SUBLANES
    body = functools.partial(_topk_body, n_heads=n_heads, hd=hd, n_blk=n_blk, top=top)
    return pl.pallas_call(
        body, grid=(1,),
        in_specs=[pl.BlockSpec(q8.shape, lambda i: (0, 0, 0)), pl.BlockSpec(kmt.shape, lambda i: (0, 0, 0))],
        out_specs=pl.BlockSpec((n, rows, LANES), lambda i: (0, 0, 0)),
        out_shape=jax.ShapeDtypeStruct((n, rows, LANES), jnp.int32),
        compiler_params=pltpu.CompilerParams(vmem_limit_bytes=VMEM_LIMIT),
        name="block_topk",
    )(q8, kmt)


def _attn_sample_body(pt_ref, sel_ref, qt_ref, knt_ref, vnt_ref, ck_ref, cv_ref, o_ref, kbuf, vbuf, sem,
                      *, n_heads, hd, n_tok, top, ppb):
    n = pl.program_id(0)
    n_sel = top * ppb

    def copies(hh, s, j):
        r, pp = divmod(j, ppb)
        blk = sel_ref[n, (hh * n_tok + s) * top + r]
        page = pt_ref[n, blk * ppb + pp]
        hs = hh * n_tok + s
        return (pltpu.make_async_copy(ck_ref.at[page, hh], kbuf.at[hs, j], sem.at[0]),
                pltpu.make_async_copy(cv_ref.at[page, hh], vbuf.at[hs, j], sem.at[1]))

    for hh in range(n_heads):
        for s in range(n_tok):
            for j in range(n_sel):
                ck, cv = copies(hh, s, j)
                ck.start()
                cv.start()
    for hh in range(n_heads):
        for s in range(n_tok):
            for j in range(n_sel):
                ck, cv = copies(hh, s, j)
                ck.wait()
                cv.wait()

    lane = lax.broadcasted_iota(jnp.int32, (1, LANES), 1)
    lane_o = lax.broadcasted_iota(jnp.int32, (hd, LANES), 1)
    for hh in range(n_heads):
        rs = slice(hh * hd, (hh + 1) * hd)
        knt = knt_ref[0, rs, :]
        vnt = vnt_ref[0, rs, :]
        out_h = jnp.zeros((hd, LANES), F32)
        for s in range(n_tok):
            hs = hh * n_tok + s
            qb = jnp.broadcast_to(qt_ref[0, rs, s:s + 1], (hd, LANES))
            logits = [jnp.sum(kbuf[hs, j] * qb, axis=0, keepdims=True) for j in range(n_sel)]
            own = jnp.sum(knt * qb, axis=0, keepdims=True)
            logits.append(jnp.where(lane <= s, own, NEG))
            mx = logits[0]
            for lg in logits[1:]:
                mx = jnp.maximum(mx, lg)
            m = jnp.max(mx, axis=-1, keepdims=True)
            ps = [jnp.exp(lg - m) for lg in logits]
            tot = ps[0]
            for pj in ps[1:]:
                tot = tot + pj
            l = jnp.sum(tot, axis=-1, keepdims=True)
            acc = vnt * ps[n_sel]
            for j in range(n_sel):
                acc = acc + vbuf[hs, j] * ps[j]
            col = jnp.sum(acc, axis=-1, keepdims=True) / l
            out_h = jnp.where(lane_o == s, col, out_h)
        o_ref[0, rs, :] = out_h


def _attn_sample_call(page_table, sel, qt, knt, vnt, ck_t, cv_t, *, n_heads, hd, n_tok, top):
    n = page_table.shape[0]
    page = ck_t.shape[-1]
    ppb = MOBA_BLOCK // page
    d_att = n_heads * hd
    body = functools.partial(_attn_sample_body, n_heads=n_heads, hd=hd, n_tok=n_tok, top=top, ppb=ppb)
    blk = pl.BlockSpec((1, d_att, LANES), lambda i, pt, sl: (i, 0, 0))
    grid_spec = pltpu.PrefetchScalarGridSpec(
        num_scalar_prefetch=2, grid=(n,),
        in_specs=[blk, blk, blk, pl.BlockSpec(memory_space=pl.ANY), pl.BlockSpec(memory_space=pl.ANY)],
        out_specs=blk,
        scratch_shapes=[pltpu.VMEM((n_heads * n_tok, top * ppb, hd, page), F32),
                        pltpu.VMEM((n_heads * n_tok, top * ppb, hd, page), F32),
                        pltpu.SemaphoreType.DMA((2,))],
    )
    return pl.pallas_call(
        body, grid_spec=grid_spec,
        out_shape=jax.ShapeDtypeStruct((n, d_att, LANES), F32),
        compiler_params=pltpu.CompilerParams(dimension_semantics=("arbitrary",), vmem_limit_bytes=VMEM_LIMIT),
        name="moba_sample",
    )(page_table, sel, qt, knt, vnt, ck_t, cv_t)


def _layer_weights(l, g_norm_mix, w_in, g_q, g_k, w_pool_group, pool_scale, w_branch_pool, w_branch_attn, w_out,
                   g_norm_ffn, w_up, w_conv, b_conv, w_down, n_heads, hd):
    d = w_in.shape[1]
    pw = w_pool_group.shape[1] * w_pool_group.shape[2]
    d_att = n_heads * hd
    wi = w_in[l].astype(BF16)
    c0, c1, c2, c3 = pw, pw + d_att, pw + 2 * d_att, pw + 3 * d_att
    blk = jnp.arange(MXU_DIM) // hd
    return {
        "g_mix": g_norm_mix[l].reshape(1, d),
        "w_u": wi[:, :c0], "w_q": wi[:, c0:c1], "w_kt": wi[:, c1:c2].T, "w_vt": wi[:, c2:c3].T, "w_g": wi[:, c3:],
        "g_q": g_q[l].reshape(1, d_att), "g_k": g_k[l].reshape(d_att, 1),
        "bd": jnp.where(blk[:, None] == blk[None, :], 1.0 / hd, 0.0).astype(BF16),
        "w_pg": w_pool_group[l].astype(BF16), "p_scale": pool_scale[l].reshape(1, pw),
        "w_bp": w_branch_pool[l].astype(BF16),
        "w_ba": w_branch_attn[l].astype(BF16), "w_out": w_out[l].astype(BF16),
        "g_ffn": g_norm_ffn[l].reshape(1, d), "w_up": w_up[l].astype(BF16),
        "w_conv": w_conv[l], "b_conv": b_conv[l].reshape(1, -1), "w_down": w_down[l].astype(BF16),
    }


def _ffn_chunk(d_ff):
    for k in (2, 4, 11, 22):
        if d_ff % k == 0 and (d_ff // k) % LANES == 0 and d_ff // k <= 1536:
            return d_ff // k
    return d_ff


def kernel(x_prompt, x_sample, cache_k, cache_v, state_pool, state_ffn_conv, page_table, c_prompt, c_sample, w_ada, b_ada, g_norm_mix, w_in, g_q, g_k, w_pool_group, pool_scale, w_branch_pool, w_branch_attn, w_out, g_norm_ffn, w_up, w_conv, b_conv, w_down):
    nb_p, seq, d = x_prompt.shape
    nb_s, n_tok, _ = x_sample.shape
    depth, n_phys, page, n_heads, hd = cache_k.shape
    d_att = n_heads * hd
    pw = state_pool.shape[-1]
    d_ff = w_down.shape[1]
    n_pages = page_table.shape[1]
    past = n_pages * page
    assert MOBA_BLOCK % page == 0 and past % MOBA_BLOCK == 0 and seq % MOBA_BLOCK == 0
    assert n_pages % _KM_SLOTS == 0 and n_tok <= SUBLANES
    n_blk = past // MOBA_BLOCK
    top = min(MOBA_TOPK, n_blk)
    assert 0 < n_blk <= LANES and seq // MOBA_BLOCK <= SUBLANES
    tm = 256
    fc = _ffn_chunk(d_ff)
    rows_s = nb_s * n_tok

    y_p, y_s = x_prompt, x_sample.transpose(1, 0, 2).reshape(1, rows_s, d)
    zero_pool = jnp.zeros((nb_p, 2 * SUBLANES, pw), F32)
    zero_conv = jnp.zeros((nb_p, SUBLANES, d_ff), F32)
    outs = [[] for _ in range(8)]
    for l in range(depth):
        wts = _layer_weights(l, g_norm_mix, w_in, g_q, g_k, w_pool_group, pool_scale, w_branch_pool, w_branch_attn,
                             w_out, g_norm_ffn, w_up, w_conv, b_conv, w_down, n_heads, hd)
        mod = _ada_call(jnp.concatenate([c_prompt, c_sample], axis=0), w_ada[l].astype(BF16), b_ada[l])
        mod_p = mod[:nb_p].reshape(nb_p, N_MOD, 1, d)
        mod_s = jnp.tile(mod[nb_p:].reshape(nb_s, N_MOD, d), (n_tok, 1, 1)).transpose(1, 0, 2)[None]

        kt, vt, q, ag, gb, pool_p = _inproj_call(y_p, mod_p, zero_pool, wts, tm=tm, rows_per_tok=1,
                                                 halo=2 * SUBLANES, pos0=0, n_heads=n_heads, hd=hd)
        o = _attn_prompt_call(q, kt, vt, n_heads=n_heads, hd=hd)
        y_p, conv_p = _post_call(y_p, o, ag, gb, mod_p, zero_conv, wts, tm=tm, rows_per_tok=1, halo=SUBLANES, fc=fc)
        to_rows = lambda a: a.reshape(nb_p, n_heads, hd, seq).transpose(0, 3, 1, 2)
        outs[0].append(to_rows(kt)); outs[1].append(to_rows(vt)); outs[2].append(pool_p); outs[3].append(conv_p)

        pool_state = state_pool[l].transpose(1, 0, 2).reshape(1, POOL_CTX * nb_s, pw)
        conv_state = state_ffn_conv[l].transpose(1, 0, 2).reshape(1, (CONV_W - 1) * nb_s, d_ff)
        kt_s, vt_s, q_s, ag_s, gb_s, pool_s = _inproj_call(y_s, mod_s, pool_state, wts, tm=rows_s, rows_per_tok=nb_s,
                                                           halo=POOL_CTX * nb_s, pos0=past, n_heads=n_heads, hd=hd)
        ck_t = cache_k[l].transpose(0, 2, 3, 1)
        cv_t = cache_v[l].transpose(0, 2, 3, 1)
        kmt = _kmean_call(page_table, ck_t.reshape(n_phys, d_att, page))
        q_ns = q_s.reshape(n_tok, nb_s, d_att).transpose(1, 0, 2).astype(F32)
        q8 = jnp.pad(q_ns, ((0, 0), (0, SUBLANES - n_tok), (0, 0)))
        sel = _topk_call(q8, kmt, n_heads=n_heads, hd=hd, n_blk=n_blk, top=top)
        sel = sel.reshape(nb_s, n_heads, SUBLANES, LANES)[:, :, :n_tok, :top].reshape(nb_s, n_heads * n_tok * top)
        lane_pad = lambda a: jnp.pad(a.reshape(d_att, n_tok, nb_s).transpose(2, 0, 1),
                                     ((0, 0), (0, 0), (0, LANES - n_tok)))
        qt = jnp.pad(q_ns.transpose(0, 2, 1), ((0, 0), (0, 0), (0, LANES - n_tok)))
        ot = _attn_sample_call(page_table, sel, qt, lane_pad(kt_s[0]), lane_pad(vt_s[0]), ck_t, cv_t,
                               n_heads=n_heads, hd=hd, n_tok=n_tok, top=top)
        o_s = ot[:, :, :n_tok].transpose(2, 0, 1).reshape(1, rows_s, d_att).astype(BF16)
        y_s, conv_s = _post_call(y_s, o_s, ag_s, gb_s, mod_s, conv_state, wts, tm=rows_s, rows_per_tok=nb_s,
                                 halo=(CONV_W - 1) * nb_s, fc=fc)
        to_rows_s = lambda a: a[0].reshape(n_heads, hd, n_tok, nb_s).transpose(3, 2, 0, 1)
        outs[4].append(to_rows_s(kt_s)); outs[5].append(to_rows_s(vt_s))
        outs[6].append(pool_s.reshape(POOL_CTX, nb_s, pw).transpose(1, 0, 2))
        outs[7].append(conv_s.reshape(CONV_W - 1, nb_s, d_ff).transpose(1, 0, 2))

    y_s_out = y_s.reshape(n_tok, nb_s, d).transpose(1, 0, 2)
    st = [jnp.stack(v) for v in outs]
    return (y_p, y_s_out, st[0], st[1], st[2], st[3], st[4], st[5], st[6], st[7])
```

```python
import functools

import jax
import jax.numpy as jnp
from jax import lax
from jax.experimental import pallas as pl
from jax.experimental.pallas import tpu as pltpu

F32 = jnp.float32
BF16 = jnp.bfloat16

MOBA_BLOCK = 256
MOBA_TOPK = 3
POOL_WINDOWS = (2, 4, 8, 16)
POOL_CTX = max(POOL_WINDOWS) - 1
CONV_W = 3
N_MOD = 6
EPS = 1e-6
NEG = -1e30
LOG2E = 1.4426950408889634

LANES = 128
SUBLANES = 8
MXU_DIM = 256
VMEM_LIMIT = 56 * 1024 * 1024
_EXP_ROWS = 64
_EXP_DEPTH = 1


def _const_spec(shape):
    nd = len(shape)
    return pl.BlockSpec(shape, lambda *_: (0,) * nd, pipeline_mode=pl.Buffered(1))


def _dot(a, b):
    return jnp.dot(a, b, preferred_element_type=F32)


def _dot_nt(a, b):
    return lax.dot_general(a, b, (((1,), (1,)), ((), ())), preferred_element_type=F32)


def _sigmoid(x):
    return 1.0 / (1.0 + jnp.exp(-x))


def _ada_body(c_ref, w_ref, b_ref, o_ref):
    c = c_ref[...]
    a = (c * _sigmoid(c)).astype(BF16)
    o_ref[...] = _dot(a, w_ref[...]) + b_ref[...]


def _ada_call(c, w_ada, b_ada):
    n, d = c.shape
    dm = w_ada.shape[1]
    bn = d
    return pl.pallas_call(
        _ada_body,
        grid=(dm // bn,),
        in_specs=[pl.BlockSpec((n, d), lambda j: (0, 0)),
                  pl.BlockSpec((d, bn), lambda j: (0, j)),
                  pl.BlockSpec((1, bn), lambda j: (0, j))],
        out_specs=pl.BlockSpec((n, bn), lambda j: (0, j)),
        out_shape=jax.ShapeDtypeStruct((n, dm), F32),
        name="ada_mod",
    )(c, w_ada, b_ada.reshape(1, dm))


def _inproj_body(x_ref, mod_ref, gmix_ref, wu_ref, wq_ref, wkt_ref, wvt_ref, wg_ref, gq_ref, gk_ref, bd_ref,
                 wpg_ref, psc_ref, wbp_ref, st_ref,
                 kt_ref, vt_ref, q_ref, k2_ref, kmt_ref, ag_ref, gb_ref, pool_ref, ubuf,
                 *, tm, rows_per_tok, halo, pos0, nt, n_heads, hd, scale, block_means):
    t = pl.program_id(1)
    d_att = n_heads * hd
    d_model = x_ref.shape[-1]
    x = x_ref[0]
    sh_m = mod_ref[0, 0]
    sc_m = mod_ref[0, 1]
    ms = jnp.mean(x * x, axis=-1, keepdims=True)
    h = x * lax.rsqrt(ms + EPS) * gmix_ref[...]
    h = (h * (1.0 + sc_m) + sh_m).astype(BF16)

    kt = _dot_nt(wkt_ref[...], h)
    vt_ref[0] = _dot_nt(wvt_ref[...], h)
    kn = []
    for hh in range(n_heads):
        rs = slice(hh * hd, (hh + 1) * hd)
        kh = kt[rs]
        msk = jnp.sum(kh * kh, axis=0, keepdims=True) * (1.0 / hd)
        kn.append(kh * lax.rsqrt(msk + EPS) * gk_ref[rs])
        kt_ref[0, rs, :] = kn[hh]
    kn = jnp.concatenate(kn, axis=0)
    k2_ref[0] = jnp.transpose(kn).astype(BF16)

    @pl.when(t == 0)
    def _():
        kmt_ref[0] = jnp.zeros((d_att, LANES), F32)

    if block_means:
        lane_b = lax.broadcasted_iota(jnp.int32, (d_att, LANES), 1)
        for bb in range(tm // MOBA_BLOCK):
            part = kn[:, bb * MOBA_BLOCK:bb * MOBA_BLOCK + LANES]
            for c in range(1, MOBA_BLOCK // LANES):
                part = part + kn[:, bb * MOBA_BLOCK + c * LANES:bb * MOBA_BLOCK + (c + 1) * LANES]
            col = jnp.sum(part, axis=-1, keepdims=True) * (1.0 / MOBA_BLOCK)
            kmt_ref[0] = jnp.where(lane_b == t * (tm // MOBA_BLOCK) + bb, col, kmt_ref[0])

    q = _dot(h, wq_ref[...])
    qq = q * q
    hi = qq.astype(BF16)
    lo = (qq - hi.astype(F32)).astype(BF16)
    parts = []
    for c in range(d_att // MXU_DIM):
        cs = slice(c * MXU_DIM, (c + 1) * MXU_DIM)
        parts.append(_dot(hi[:, cs], bd_ref[...]) + _dot(lo[:, cs], bd_ref[...]))
    msq = jnp.concatenate(parts, axis=1)
    q_ref[0] = (q * lax.rsqrt(msq + EPS) * gq_ref[...] * scale).astype(BF16)

    u = _dot(h, wu_ref[...])

    @pl.when(t == 0)
    def _():
        ubuf[0:halo, :] = st_ref[0]

    @pl.when(t > 0)
    def _():
        ubuf[0:halo, :] = ubuf[tm:tm + halo, :]

    ubuf[halo:halo + tm, :] = u
    row = lax.broadcasted_iota(jnp.int32, (tm, 1), 0) + t * tm
    gw = LANES
    ys = []
    for g, w in enumerate(POOL_WINDOWS):
        cs = slice(g * gw, (g + 1) * gw)
        ug = u[:, cs]
        acc = ug
        cnt = 1.0
        for j in range(1, w):
            off = halo - j * rows_per_tok
            acc = acc + ubuf[off:off + tm, cs]
            first_row = (j - pos0) * rows_per_tok
            if first_row <= 0:
                cnt = cnt + 1.0
            else:
                cnt = cnt + (row >= first_row).astype(F32)
        pooled = (acc / cnt - ug).astype(BF16)
        ys.append(_dot(pooled, wpg_ref[g]))
    y = jnp.concatenate(ys, axis=1) * psc_ref[...]
    a = _dot(y.astype(BF16), wbp_ref[...])

    gates = _dot(h, wg_ref[...])
    ag_ref[0] = (_sigmoid(gates[:, :d_model]) * a).astype(BF16)
    gb_ref[0] = _sigmoid(gates[:, d_model:]).astype(BF16)

    @pl.when(t == nt - 1)
    def _():
        keep = POOL_CTX * rows_per_tok
        pool_ref[0] = ubuf[halo + tm - keep:halo + tm, :]


def _inproj_call(x, mod, state, wts, *, tm, rows_per_tok, halo, pos0, n_heads, hd, block_means):
    n, s, d = x.shape
    nt = s // tm
    d_att = n_heads * hd
    pw = wts["w_u"].shape[1]
    keep = POOL_CTX * rows_per_tok
    body = functools.partial(_inproj_body, tm=tm, rows_per_tok=rows_per_tok, halo=halo, pos0=pos0, nt=nt,
                             n_heads=n_heads, hd=hd, scale=hd ** -0.5 * LOG2E, block_means=block_means)
    rm = mod.shape[2]
    in_specs = [
        pl.BlockSpec((1, tm, d), lambda i, t: (i, t, 0)),
        pl.BlockSpec((1, N_MOD, rm, d), lambda i, t: (i, 0, 0, 0)),
        _const_spec((1, d)),
        _const_spec(wts["w_u"].shape), _const_spec(wts["w_q"].shape), _const_spec(wts["w_kt"].shape),
        _const_spec(wts["w_vt"].shape), _const_spec(wts["w_g"].shape),
        _const_spec((1, d_att)), _const_spec((d_att, 1)), _const_spec((MXU_DIM, MXU_DIM)),
        _const_spec(wts["w_pg"].shape), _const_spec((1, pw)), _const_spec(wts["w_bp"].shape),
        pl.BlockSpec((1, halo, pw), lambda i, t: (i, 0, 0)),
    ]
    out_specs = [
        pl.BlockSpec((1, d_att, tm), lambda i, t: (i, 0, t)),
        pl.BlockSpec((1, d_att, tm), lambda i, t: (i, 0, t)),
        pl.BlockSpec((1, tm, d_att), lambda i, t: (i, t, 0)),
        pl.BlockSpec((1, tm, d_att), lambda i, t: (i, t, 0)),
        pl.BlockSpec((1, d_att, LANES), lambda i, t: (i, 0, 0)),
        pl.BlockSpec((1, tm, d), lambda i, t: (i, t, 0)),
        pl.BlockSpec((1, tm, d), lambda i, t: (i, t, 0)),
        pl.BlockSpec((1, keep, pw), lambda i, t: (i, 0, 0)),
    ]
    out_shape = [
        jax.ShapeDtypeStruct((n, d_att, s), F32),
        jax.ShapeDtypeStruct((n, d_att, s), F32),
        jax.ShapeDtypeStruct((n, s, d_att), BF16),
        jax.ShapeDtypeStruct((n, s, d_att), BF16),
        jax.ShapeDtypeStruct((n, d_att, LANES), F32),
        jax.ShapeDtypeStruct((n, s, d), BF16),
        jax.ShapeDtypeStruct((n, s, d), BF16),
        jax.ShapeDtypeStruct((n, keep, pw), F32),
    ]
    return pl.pallas_call(
        body, grid=(n, nt), in_specs=in_specs, out_specs=out_specs, out_shape=out_shape,
        scratch_shapes=[pltpu.VMEM((halo + tm, pw), F32)],
        compiler_params=pltpu.CompilerParams(dimension_semantics=("arbitrary", "arbitrary"),
                                             vmem_limit_bytes=VMEM_LIMIT),
        name="mixer_inproj",
    )(x, mod, wts["g_mix"], wts["w_u"], wts["w_q"], wts["w_kt"], wts["w_vt"], wts["w_g"], wts["g_q"], wts["g_k"],
      wts["bd"], wts["w_pg"], wts["p_scale"], wts["w_bp"], state)


def _attn_prompt_body(q_ref, k2_ref, kmt_ref, vt_ref, o_ref, va_ref, qh_ref, pen_ref, s_ref, p_ref,
                      *, seq, nb, hd):
    blk = MOBA_BLOCK
    nbp = SUBLANES
    q2 = q_ref[0]
    ones_rows = (lax.broadcasted_iota(jnp.int32, (SUBLANES, seq), 0) == 0).astype(F32)
    for hh in range(2):
        va = jnp.concatenate([vt_ref[0, hh * hd:(hh + 1) * hd, :], ones_rows, jnp.zeros((SUBLANES, seq), F32)], axis=0)
        va_ref[hh] = va.astype(BF16)
    km = jnp.transpose(kmt_ref[0])
    lane_q = lax.broadcasted_iota(jnp.int32, (seq, 2 * hd), 1)
    lane_m = lax.broadcasted_iota(jnp.int32, (2 * SUBLANES, 2 * hd), 1)
    tpos = lax.broadcasted_iota(jnp.int32, (nbp, seq), 1)
    bidx = lax.broadcasted_iota(jnp.int32, (nbp, seq), 0)
    qblk = jnp.zeros((nbp, seq), jnp.int32)
    for b in range(1, nb):
        qblk = qblk + (tpos >= b * blk).astype(jnp.int32)
    for hh in range(2):
        in_head = (lane_m >= hh * hd) & (lane_m < (hh + 1) * hd)
        kmh = jnp.where(in_head, km[0:2 * SUBLANES], 0.0).astype(BF16)
        sct = _dot_nt(kmh, q2)[0:nbp]
        rank = jnp.zeros((nbp, seq), F32)
        for bp in range(nb):
            rowv = sct[bp:bp + 1, :]
            beats = (rowv > sct) | ((rowv == sct) & (bp < bidx))
            rank = rank + jnp.where(beats & (bp < qblk), 1.0, 0.0)
        keep = (bidx < qblk) & (rank < MOBA_TOPK)
        pen_ref[hh] = jnp.where(keep, 0.0, NEG)
        own_lanes = (lane_q >= hh * hd) & (lane_q < (hh + 1) * hd)
        qh_ref[hh] = jnp.where(own_lanes, q2, jnp.zeros_like(q2))

    kr = lax.broadcasted_iota(jnp.int32, (blk, blk), 0)
    qc = lax.broadcasted_iota(jnp.int32, (blk, blk), 1)
    causal = kr <= qc
    items = [(i, hh) for i in range(nb) for hh in range(2)]
    m_of, zero_of, o_of = {}, {}, {}

    def base_of(i):
        return blk * (i * (i + 1) // 2)

    def logits_piece(k, j):
        i, hh = items[k]
        qs = slice(i * blk, (i + 1) * blk)
        sv = _dot_nt(k2_ref[0, j * blk:(j + 1) * blk, :], qh_ref[hh, qs, :])
        s_ref[hh, base_of(i) + j * blk:base_of(i) + (j + 1) * blk, :] = sv
        if j == i:
            mj = jnp.max(jnp.where(causal, sv, NEG), axis=0, keepdims=True)
        else:
            mj = jnp.max(sv, axis=0, keepdims=True) + pen_ref[hh, j:j + 1, qs]
        m_of[k] = mj if j == 0 else jnp.maximum(m_of[k], mj)

    def exp_piece(k, j):
        i, hh = items[k]
        qs = slice(i * blk, (i + 1) * blk)
        if j == 0:
            zero_of[k] = [jnp.zeros((1, blk), F32)] * _EXP_DEPTH
        shift = m_of[k] if j == i else m_of[k] - pen_ref[hh, j:j + 1, qs]
        for c in range(blk // _EXP_ROWS):
            r0 = base_of(i) + j * blk + c * _EXP_ROWS
            x = s_ref[hh, r0:r0 + _EXP_ROWS, :]
            if j == i:
                x = jnp.where(causal[c * _EXP_ROWS:(c + 1) * _EXP_ROWS], x, NEG)
            p = jnp.exp2(x - (shift + zero_of[k][c % _EXP_DEPTH]))
            p_ref[hh, r0:r0 + _EXP_ROWS, :] = p.astype(BF16)
            zero_of[k][c % _EXP_DEPTH] = jnp.minimum(p[_EXP_ROWS - 1:_EXP_ROWS], 0.0)

    def values(k):
        i, hh = items[k]
        ln = (i + 1) * blk
        ot = _dot(va_ref[hh, :, 0:ln], p_ref[hh, base_of(i):base_of(i) + ln, :])
        o_of[k] = ot[0:hd] / ot[hd:hd + 1]
        if hh == 1:
            both = jnp.concatenate([o_of.pop(k - 1), o_of.pop(k)], axis=0)
            o_ref[0, i * blk:(i + 1) * blk, :] = jnp.transpose(both).astype(BF16)

    for j in range(items[0][0] + 1):
        logits_piece(0, j)
    for k in range(len(items)):
        ahead = [(k + 1, j) for j in range(items[k + 1][0] + 1)] if k + 1 < len(items) else []
        for j in range(items[k][0] + 1):
            exp_piece(k, j)
            if ahead:
                logits_piece(*ahead.pop(0))
        for kj in ahead:
            logits_piece(*kj)
        values(k)


def _attn_prompt_call(q, k2, kmt, vt, *, n_heads, hd):
    n, s, d_att = q.shape
    nb = s // MOBA_BLOCK
    body = functools.partial(_attn_prompt_body, seq=s, nb=nb, hd=hd)
    tri = MOBA_BLOCK * (nb * (nb + 1) // 2)
    tok_spec = pl.BlockSpec((1, s, 2 * hd), lambda i, p: (i, 0, p))
    return pl.pallas_call(
        body, grid=(n, n_heads // 2),
        in_specs=[tok_spec, tok_spec,
                  pl.BlockSpec((1, 2 * hd, LANES), lambda i, p: (i, p, 0)),
                  pl.BlockSpec((1, 2 * hd, s), lambda i, p: (i, p, 0))],
        out_specs=tok_spec,
        out_shape=jax.ShapeDtypeStruct((n, s, d_att), BF16),
        scratch_shapes=[pltpu.VMEM((2, hd + 2 * SUBLANES, s), BF16),
                        pltpu.VMEM((2, s, 2 * hd), BF16), pltpu.VMEM((2, SUBLANES, s), F32),
                        pltpu.VMEM((2, tri, MOBA_BLOCK), F32), pltpu.VMEM((2, tri, MOBA_BLOCK), BF16)],
        compiler_params=pltpu.CompilerParams(dimension_semantics=("arbitrary", "arbitrary"),
                                             vmem_limit_bytes=VMEM_LIMIT),
        name="moba_prompt",
    )(q, k2, kmt, vt)


def _post_body(x_ref, o_ref, ag_ref, gb_ref, mod_ref, wba_ref, wout_ref, gffn_ref, wup_ref, wconv_ref, bconv_ref,
               wdown_ref, cst_ref, y_ref, conv_ref, gbuf, *, tm, rows_per_tok, halo, nt, fc, d_ff):
    t = pl.program_id(1)
    x = x_ref[0]
    gt_m = mod_ref[0, 2]
    sh_f = mod_ref[0, 3]
    sc_f = mod_ref[0, 4]
    gt_f = mod_ref[0, 5]
    oa = _dot(o_ref[0], wba_ref[...])
    merged = ag_ref[0].astype(F32) + gb_ref[0].astype(F32) * oa
    x1 = x + gt_m * _dot(merged.astype(BF16), wout_ref[...])
    ms = jnp.mean(x1 * x1, axis=-1, keepdims=True)
    h2 = x1 * lax.rsqrt(ms + EPS) * gffn_ref[...]
    h2 = (h2 * (1.0 + sc_f) + sh_f).astype(BF16)

    @pl.when(t == 0)
    def _():
        gbuf[0:halo, :] = cst_ref[0]

    @pl.when(t > 0)
    def _():
        gbuf[0:halo, :] = gbuf[tm:tm + halo, :]

    acc = jnp.zeros_like(x)
    r = rows_per_tok
    for c in range(d_ff // fc):
        cs = slice(c * fc, (c + 1) * fc)
        fg = _dot(h2, wup_ref[:, cs])
        fv = _dot(h2, wup_ref[:, d_ff + c * fc:d_ff + (c + 1) * fc])
        gbuf[halo:halo + tm, cs] = fg
        conv = bconv_ref[:, cs]
        conv = conv + gbuf[halo - 2 * r:halo - 2 * r + tm, cs] * wconv_ref[0:1, cs]
        conv = conv + gbuf[halo - r:halo - r + tm, cs] * wconv_ref[1:2, cs]
        conv = conv + fg * wconv_ref[2:3, cs]
        act = (conv * _sigmoid(conv) * fv).astype(BF16)
        acc = acc + _dot(act, wdown_ref[cs, :])
    y_ref[0] = x1 + gt_f * acc

    @pl.when(t == nt - 1)
    def _():
        keep = (CONV_W - 1) * r
        conv_ref[0] = gbuf[halo + tm - keep:halo + tm, :]


def _post_call(x, o, ag, gb, mod, cstate, wts, *, tm, rows_per_tok, halo, fc):
    n, s, d = x.shape
    nt = s // tm
    d_att = o.shape[-1]
    d_ff = wts["w_down"].shape[0]
    keep = (CONV_W - 1) * rows_per_tok
    rm = mod.shape[2]
    body = functools.partial(_post_body, tm=tm, rows_per_tok=rows_per_tok, halo=halo, nt=nt, fc=fc, d_ff=d_ff)
    row_spec = lambda w: pl.BlockSpec((1, tm, w), lambda i, t: (i, t, 0))
    in_specs = [
        row_spec(d), row_spec(d_att), row_spec(d), row_spec(d),
        pl.BlockSpec((1, N_MOD, rm, d), lambda i, t: (i, 0, 0, 0)),
        _const_spec(wts["w_ba"].shape), _const_spec(wts["w_out"].shape), _const_spec((1, d)),
        _const_spec(wts["w_up"].shape), _const_spec((CONV_W, d_ff)), _const_spec((1, d_ff)),
        _const_spec(wts["w_down"].shape),
        pl.BlockSpec((1, halo, d_ff), lambda i, t: (i, 0, 0)),
    ]
    out_specs = [row_spec(d), pl.BlockSpec((1, keep, d_ff), lambda i, t: (i, 0, 0))]
    out_shape = [jax.ShapeDtypeStruct((n, s, d), F32), jax.ShapeDtypeStruct((n, keep, d_ff), F32)]
    return pl.pallas_call(
        body, grid=(n, nt), in_specs=in_specs, out_specs=out_specs, out_shape=out_shape,
        scratch_shapes=[pltpu.VMEM((halo + tm, d_ff), F32)],
        compiler_params=pltpu.CompilerParams(dimension_semantics=("arbitrary", "arbitrary"),
                                             vmem_limit_bytes=VMEM_LIMIT),
        name="merge_ffn",
    )(x, o, ag, gb, mod, wts["w_ba"], wts["w_out"], wts["g_ffn"], wts["w_up"], wts["w_conv"], wts["b_conv"],
      wts["w_down"], cstate)


_KM_GROUP = 16
_KM_ROWS = 64


def _kmean_body(pt_ref, ck_ref, km_ref, buf, sem, *, n_seq, n_pages, ppb):
    n = pl.program_id(0)
    rows = km_ref.shape[1]
    bpg = _KM_GROUP // ppb
    n_groups = n_pages // _KM_GROUP

    def copy(seq, p, slot):
        return pltpu.make_async_copy(ck_ref.at[pt_ref[seq, p]], buf.at[slot], sem.at[slot])

    def start_group(seq, g, half):
        for k in range(_KM_GROUP):
            copy(seq, g * _KM_GROUP + k, half * _KM_GROUP + k).start()

    @pl.when(n == 0)
    def _():
        start_group(0, 0, 0)
        start_group(0, 1, 1)

    km_ref[0] = jnp.zeros((rows, LANES), F32)
    lane = lax.broadcasted_iota(jnp.int32, (_KM_ROWS, LANES), 1)

    def pair(g2, carry):
        for half in range(2):
            g = g2 * 2 + half
            for k in range(_KM_GROUP):
                copy(n, g * _KM_GROUP + k, half * _KM_GROUP + k).wait()
            for rc in range(rows // _KM_ROWS):
                rs = slice(rc * _KM_ROWS, (rc + 1) * _KM_ROWS)
                upd = jnp.zeros((_KM_ROWS, LANES), F32)
                for bb in range(bpg):
                    slot = half * _KM_GROUP + bb * ppb
                    x = buf[slot, rs, :]
                    for pp in range(1, ppb):
                        x = x + buf[slot + pp, rs, :]
                    col = jnp.sum(x, axis=-1, keepdims=True)
                    upd = jnp.where(lane == g * bpg + bb, col, upd)
                km_ref[0, rs, :] = km_ref[0, rs, :] + upd * (1.0 / MOBA_BLOCK)
            nxt = g + 2

            @pl.when(nxt < n_groups)
            def _():
                start_group(n, nxt, half)

            @pl.when((nxt >= n_groups) & (n + 1 < n_seq))
            def _():
                start_group(n + 1, nxt - n_groups, half)
        return carry

    lax.fori_loop(0, n_groups // 2, pair, 0)


def _kmean_call(page_table, ck_t):
    n, n_pages = page_table.shape
    _, rows, page = ck_t.shape
    ppb = MOBA_BLOCK // page
    assert n_pages % (2 * _KM_GROUP) == 0 and rows % _KM_ROWS == 0
    body = functools.partial(_kmean_body, n_seq=n, n_pages=n_pages, ppb=ppb)
    grid_spec = pltpu.PrefetchScalarGridSpec(
        num_scalar_prefetch=1, grid=(n,),
        in_specs=[pl.BlockSpec(memory_space=pl.ANY)],
        out_specs=pl.BlockSpec((1, rows, LANES), lambda i, pt: (i, 0, 0)),
        scratch_shapes=[pltpu.VMEM((2 * _KM_GROUP, rows, page), F32), pltpu.SemaphoreType.DMA((2 * _KM_GROUP,))],
    )
    return pl.pallas_call(
        body, grid_spec=grid_spec,
        out_shape=jax.ShapeDtypeStruct((n, rows, LANES), F32),
        compiler_params=pltpu.CompilerParams(dimension_semantics=("arbitrary",), vmem_limit_bytes=VMEM_LIMIT),
        name="cache_block_means",
    )(page_table, ck_t)


def _topk_body(q_ref, km_ref, sel_ref, *, n_heads, hd, n_blk, top):
    n_seq = q_ref.shape[0]
    rows = n_heads * SUBLANES
    lane_q = lax.broadcasted_iota(jnp.int32, (SUBLANES, n_heads * hd), 1)
    lane = lax.broadcasted_iota(jnp.int32, (rows, LANES), 1).astype(F32)

    def per_seq(i, carry):
        q8 = q_ref[i]
        qbd = jnp.concatenate(
            [jnp.where((lane_q >= hh * hd) & (lane_q < (hh + 1) * hd), q8, 0.0) for hh in range(n_heads)], axis=0)
        sc = _dot(qbd.astype(BF16), km_ref[i].astype(BF16))
        sc = jnp.where(lane < n_blk, sc, -jnp.inf)
        out = jnp.zeros((rows, LANES), F32)
        for r in range(top):
            m = jnp.max(sc, axis=-1, keepdims=True)
            idx = jnp.min(jnp.where(sc == m, lane, float(LANES)), axis=-1, keepdims=True)
            out = jnp.where(lane == r, idx, out)
            sc = jnp.where(lane == idx, -jnp.inf, sc)
        sel_ref[i] = out.astype(jnp.int32)
        return carry

    lax.fori_loop(0, n_seq, per_seq, 0)


def _topk_call(q8, kmt, *, n_heads, hd, n_blk, top):
    n = q8.shape[0]
    rows = n_heads * SUBLANES
    body = functools.partial(_topk_body, n_heads=n_heads, hd=hd, n_blk=n_blk, top=top)
    return pl.pallas_call(
        body, grid=(1,),
        in_specs=[pl.BlockSpec(q8.shape, lambda i: (0, 0, 0)), pl.BlockSpec(kmt.shape, lambda i: (0, 0, 0))],
        out_specs=pl.BlockSpec((n, rows, LANES), lambda i: (0, 0, 0)),
        out_shape=jax.ShapeDtypeStruct((n, rows, LANES), jnp.int32),
        compiler_params=pltpu.CompilerParams(vmem_limit_bytes=VMEM_LIMIT),
        name="block_topk",
    )(q8, kmt)


def _attn_sample_body(pt_ref, sel_ref, qt_ref, knt_ref, vnt_ref, ck_ref, cv_ref, o_ref, kbuf, vbuf, lg_ref, p_ref, sem,
                      *, n_seq, n_heads, hd, n_tok, top, ppb):
    n = pl.program_id(0)
    n_sel = top * ppb
    n_grp = n_heads * n_tok
    slot = lax.rem(n, 2)

    def gather(seq, sl):
        for hh in range(n_heads):
            for s in range(n_tok):
                g = hh * n_tok + s
                for j in range(n_sel):
                    r, pp = divmod(j, ppb)
                    page = pt_ref[seq, sel_ref[seq, g * top + r] * ppb + pp]
                    pltpu.make_async_copy(ck_ref.at[page, hh], kbuf.at[sl, g, j], sem.at[sl]).start()
                    pltpu.make_async_copy(cv_ref.at[page, hh], vbuf.at[sl, g, j], sem.at[sl]).start()

    @pl.when(n == 0)
    def _():
        gather(0, 0)

    @pl.when(n + 1 < n_seq)
    def _():
        gather(n + 1, 1 - slot)

    for g in range(n_grp):
        for j in range(n_sel):
            pltpu.make_async_copy(ck_ref.at[0, 0], kbuf.at[slot, g, j], sem.at[slot]).wait()
            pltpu.make_async_copy(cv_ref.at[0, 0], vbuf.at[slot, g, j], sem.at[slot]).wait()

    lane = lax.broadcasted_iota(jnp.int32, (1, LANES), 1)
    pad_rows = [jnp.full((1, LANES), NEG, F32)] * (SUBLANES - n_sel - 1)
    for hh in range(n_heads):
        rs = slice(hh * hd, (hh + 1) * hd)
        knt = knt_ref[0, rs, :]
        for s in range(n_tok):
            g = hh * n_tok + s
            qb = jnp.broadcast_to(qt_ref[0, rs, s:s + 1], (hd, LANES))
            rows = [jnp.sum(kbuf[slot, g, j] * qb, axis=0, keepdims=True) for j in range(n_sel)]
            own = jnp.sum(knt * qb, axis=0, keepdims=True)
            rows.append(jnp.where(lane <= s, own, NEG))
            lg_ref[g] = jnp.concatenate(rows + pad_rows, axis=0)

    lg = lg_ref[...]
    m = jnp.max(jnp.max(lg, axis=1), axis=-1, keepdims=True)
    mb = jnp.broadcast_to(m, (n_grp, LANES))
    for g in range(n_grp):
        p_ref[g] = jnp.exp2(lg_ref[g] - mb[g:g + 1, :])
    l = jnp.sum(jnp.sum(p_ref[...], axis=1), axis=-1, keepdims=True)
    inv = jnp.broadcast_to(1.0 / l, (n_grp, LANES))

    lane_o = lax.broadcasted_iota(jnp.int32, (hd, LANES), 1)
    for hh in range(n_heads):
        rs = slice(hh * hd, (hh + 1) * hd)
        vnt = vnt_ref[0, rs, :]
        out_h = jnp.zeros((hd, LANES), F32)
        for s in range(n_tok):
            g = hh * n_tok + s
            pg = p_ref[g] * inv[g:g + 1, :]
            acc = vnt * pg[n_sel:n_sel + 1, :]
            for j in range(n_sel):
                acc = acc + vbuf[slot, g, j] * pg[j:j + 1, :]
            col = jnp.sum(acc, axis=-1, keepdims=True)
            out_h = jnp.where(lane_o == s, col, out_h)
        o_ref[0, rs, :] = out_h


def _attn_sample_call(page_table, sel, qt, knt, vnt, ck_t, cv_t, *, n_heads, hd, n_tok, top):
    n = page_table.shape[0]
    page = ck_t.shape[-1]
    ppb = MOBA_BLOCK // page
    d_att = n_heads * hd
    n_grp = n_heads * n_tok
    assert top * ppb + 1 <= SUBLANES
    body = functools.partial(_attn_sample_body, n_seq=n, n_heads=n_heads, hd=hd, n_tok=n_tok, top=top, ppb=ppb)
    blk = pl.BlockSpec((1, d_att, LANES), lambda i, pt, sl: (i, 0, 0))
    grid_spec = pltpu.PrefetchScalarGridSpec(
        num_scalar_prefetch=2, grid=(n,),
        in_specs=[blk, blk, blk, pl.BlockSpec(memory_space=pl.ANY), pl.BlockSpec(memory_space=pl.ANY)],
        out_specs=blk,
        scratch_shapes=[pltpu.VMEM((2, n_grp, top * ppb, hd, page), F32),
                        pltpu.VMEM((2, n_grp, top * ppb, hd, page), F32),
                        pltpu.VMEM((n_grp, SUBLANES, LANES), F32),
                        pltpu.VMEM((n_grp, SUBLANES, LANES), F32),
                        pltpu.SemaphoreType.DMA((2,))],
    )
    return pl.pallas_call(
        body, grid_spec=grid_spec,
        out_shape=jax.ShapeDtypeStruct((n, d_att, LANES), F32),
        compiler_params=pltpu.CompilerParams(dimension_semantics=("arbitrary",), vmem_limit_bytes=VMEM_LIMIT),
        name="moba_sample",
    )(page_table, sel, qt, knt, vnt, ck_t, cv_t)


def _layer_weights(l, g_norm_mix, w_in, g_q, g_k, w_pool_group, pool_scale, w_branch_pool, w_branch_attn, w_out,
                   g_norm_ffn, w_up, w_conv, b_conv, w_down, n_heads, hd):
    d = w_in.shape[1]
    pw = w_pool_group.shape[1] * w_pool_group.shape[2]
    d_att = n_heads * hd
    wi = w_in[l].astype(BF16)
    c0, c1, c2, c3 = pw, pw + d_att, pw + 2 * d_att, pw + 3 * d_att
    blk = jnp.arange(MXU_DIM) // hd
    return {
        "g_mix": g_norm_mix[l].reshape(1, d),
        "w_u": wi[:, :c0], "w_q": wi[:, c0:c1], "w_kt": wi[:, c1:c2].T, "w_vt": wi[:, c2:c3].T, "w_g": wi[:, c3:],
        "g_q": g_q[l].reshape(1, d_att), "g_k": g_k[l].reshape(d_att, 1),
        "bd": jnp.where(blk[:, None] == blk[None, :], 1.0 / hd, 0.0).astype(BF16),
        "w_pg": w_pool_group[l].astype(BF16), "p_scale": pool_scale[l].reshape(1, pw),
        "w_bp": w_branch_pool[l].astype(BF16),
        "w_ba": w_branch_attn[l].astype(BF16), "w_out": w_out[l].astype(BF16),
        "g_ffn": g_norm_ffn[l].reshape(1, d), "w_up": w_up[l].astype(BF16),
        "w_conv": w_conv[l], "b_conv": b_conv[l].reshape(1, -1), "w_down": w_down[l].astype(BF16),
    }


def _ffn_chunk(d_ff):
    for k in (2, 4, 11, 22):
        if d_ff % k == 0 and (d_ff // k) % LANES == 0 and d_ff // k <= 1536:
            return d_ff // k
    return d_ff


def kernel(x_prompt, x_sample, cache_k, cache_v, state_pool, state_ffn_conv, page_table, c_prompt, c_sample, w_ada, b_ada, g_norm_mix, w_in, g_q, g_k, w_pool_group, pool_scale, w_branch_pool, w_branch_attn, w_out, g_norm_ffn, w_up, w_conv, b_conv, w_down):
    nb_p, seq, d = x_prompt.shape
    nb_s, n_tok, _ = x_sample.shape
    depth, n_phys, page, n_heads, hd = cache_k.shape
    d_att = n_heads * hd
    pw = state_pool.shape[-1]
    d_ff = w_down.shape[1]
    n_pages = page_table.shape[1]
    past = n_pages * page
    assert MOBA_BLOCK % page == 0 and past % MOBA_BLOCK == 0 and seq % MOBA_BLOCK == 0
    assert n_tok <= SUBLANES
    n_blk = past // MOBA_BLOCK
    top = min(MOBA_TOPK, n_blk)
    assert 0 < n_blk <= LANES and seq // MOBA_BLOCK <= SUBLANES
    tm = 256
    fc = _ffn_chunk(d_ff)
    rows_s = nb_s * n_tok

    y_p, y_s = x_prompt, x_sample.transpose(1, 0, 2).reshape(1, rows_s, d)
    zero_pool = jnp.zeros((nb_p, 2 * SUBLANES, pw), F32)
    zero_conv = jnp.zeros((nb_p, SUBLANES, d_ff), F32)
    outs = [[] for _ in range(8)]
    for l in range(depth):
        wts = _layer_weights(l, g_norm_mix, w_in, g_q, g_k, w_pool_group, pool_scale, w_branch_pool, w_branch_attn,
                             w_out, g_norm_ffn, w_up, w_conv, b_conv, w_down, n_heads, hd)
        mod = _ada_call(jnp.concatenate([c_prompt, c_sample], axis=0), w_ada[l].astype(BF16), b_ada[l])
        mod_p = mod[:nb_p].reshape(nb_p, N_MOD, 1, d)
        mod_s = jnp.tile(mod[nb_p:].reshape(nb_s, N_MOD, d), (n_tok, 1, 1)).transpose(1, 0, 2)[None]

        kt, vt, q, k2, kmt, ag, gb, pool_p = _inproj_call(y_p, mod_p, zero_pool, wts, tm=tm, rows_per_tok=1,
                                                          halo=2 * SUBLANES, pos0=0, n_heads=n_heads, hd=hd,
                                                          block_means=True)
        o = _attn_prompt_call(q, k2, kmt, vt, n_heads=n_heads, hd=hd)
        y_p, conv_p = _post_call(y_p, o, ag, gb, mod_p, zero_conv, wts, tm=tm, rows_per_tok=1, halo=SUBLANES, fc=fc)
        to_rows = lambda a: a.reshape(nb_p, n_heads, hd, seq).transpose(0, 3, 1, 2)
        outs[0].append(to_rows(kt)); outs[1].append(to_rows(vt)); outs[2].append(pool_p); outs[3].append(conv_p)

        pool_state = state_pool[l].transpose(1, 0, 2).reshape(1, POOL_CTX * nb_s, pw)
        conv_state = state_ffn_conv[l].transpose(1, 0, 2).reshape(1, (CONV_W - 1) * nb_s, d_ff)
        kt_s, vt_s, q_s, _, _, ag_s, gb_s, pool_s = _inproj_call(
            y_s, mod_s, pool_state, wts, tm=rows_s, rows_per_tok=nb_s, halo=POOL_CTX * nb_s, pos0=past,
            n_heads=n_heads, hd=hd, block_means=False)
        ck_t = cache_k[l].transpose(0, 2, 3, 1)
        cv_t = cache_v[l].transpose(0, 2, 3, 1)
        kmt_s = _kmean_call(page_table, ck_t.reshape(n_phys, d_att, page))
        q_ns = q_s.reshape(n_tok, nb_s, d_att).transpose(1, 0, 2).astype(F32)
        q8 = jnp.pad(q_ns, ((0, 0), (0, SUBLANES - n_tok), (0, 0)))
        sel = _topk_call(q8, kmt_s, n_heads=n_heads, hd=hd, n_blk=n_blk, top=top)
        sel = sel.reshape(nb_s, n_heads, SUBLANES, LANES)[:, :, :n_tok, :top].reshape(nb_s, n_heads * n_tok * top)
        lane_pad = lambda a: jnp.pad(a.reshape(d_att, n_tok, nb_s).transpose(2, 0, 1),
                                     ((0, 0), (0, 0), (0, LANES - n_tok)))
        qt = jnp.pad(q_ns.transpose(0, 2, 1), ((0, 0), (0, 0), (0, LANES - n_tok)))
        ot = _attn_sample_call(page_table, sel, qt, lane_pad(kt_s[0]), lane_pad(vt_s[0]), ck_t, cv_t,
                               n_heads=n_heads, hd=hd, n_tok=n_tok, top=top)
        o_s = ot[:, :, :n_tok].transpose(2, 0, 1).reshape(1, rows_s, d_att).astype(BF16)
        y_s, conv_s = _post_call(y_s, o_s, ag_s, gb_s, mod_s, conv_state, wts, tm=rows_s, rows_per_tok=nb_s,
                                 halo=(CONV_W - 1) * nb_s, fc=fc)
        to_rows_s = lambda a: a[0].reshape(n_heads, hd, n_tok, nb_s).transpose(3, 2, 0, 1)
        outs[4].append(to_rows_s(kt_s)); outs[5].append(to_rows_s(vt_s))
        outs[6].append(pool_s.reshape(POOL_CTX, nb_s, pw).transpose(1, 0, 2))
        outs[7].append(conv_s.reshape(CONV_W - 1, nb_s, d_ff).transpose(1, 0, 2))

    y_s_out = y_s.reshape(n_tok, nb_s, d).transpose(1, 0, 2)
    st = [jnp.stack(v) for v in outs]
    return (y_p, y_s_out, st[0], st[1], st[2], st[3], st[4], st[5], st[6], st[7])
```

```python
import functools

import jax
import jax.numpy as jnp
from jax import lax
from jax.experimental import pallas as pl
from jax.experimental.pallas import tpu as pltpu

F32 = jnp.float32
BF16 = jnp.bfloat16

MOBA_BLOCK = 256
MOBA_TOPK = 3
POOL_WINDOWS = (2, 4, 8, 16)
POOL_CTX = max(POOL_WINDOWS) - 1
CONV_W = 3
N_MOD = 6
EPS = 1e-6
NEG = -1e30
LOG2E = 1.4426950408889634

LANES = 128
SUBLANES = 8
MXU_DIM = 256
VMEM_LIMIT = 56 * 1024 * 1024
_EXP_ROWS = 64
_EXP_DEPTH = 1
_KM_GROUP = 16
_KM_ROWS = 64


def _const_spec(shape):
    nd = len(shape)
    return pl.BlockSpec(shape, lambda *_: (0,) * nd, pipeline_mode=pl.Buffered(1))


def _dot(a, b):
    return jnp.dot(a, b, preferred_element_type=F32)


def _dot_nt(a, b):
    return lax.dot_general(a, b, (((1,), (1,)), ((), ())), preferred_element_type=F32)


def _sigmoid(x):
    return 1.0 / (1.0 + jnp.exp(-x))


def _ada_body(c_ref, w_ref, b_ref, o_ref):
    c = c_ref[...]
    a = (c * _sigmoid(c)).astype(BF16)
    o_ref[...] = _dot(a, w_ref[...]) + b_ref[...]


def _ada_call(c, w_ada, b_ada):
    n, d = c.shape
    dm = w_ada.shape[1]
    bn = d
    return pl.pallas_call(
        _ada_body,
        grid=(dm // bn,),
        in_specs=[pl.BlockSpec((n, d), lambda j: (0, 0)),
                  pl.BlockSpec((d, bn), lambda j: (0, j)),
                  pl.BlockSpec((1, bn), lambda j: (0, j))],
        out_specs=pl.BlockSpec((n, bn), lambda j: (0, j)),
        out_shape=jax.ShapeDtypeStruct((n, dm), F32),
        name="ada_mod",
    )(c, w_ada, b_ada.reshape(1, dm))


def _inproj_body(*refs, tm, rows_per_tok, halo, pos0, nt, n_heads, hd, scale, block_means, stream):
    if stream is not None:
        pt_ref, refs = refs[0], refs[1:]
    (x_ref, mod_ref, gmix_ref, wu_ref, wq_ref, wkt_ref, wvt_ref, wg_ref, gq_ref, gk_ref, bd_ref, wpg_ref, psc_ref,
     wbp_ref, st_ref) = refs[:15]
    refs = refs[15:]
    if stream is not None:
        ck_ref, refs = refs[0], refs[1:]
    kt_ref, vt_ref, q_ref, k2_ref, kmt_ref, ag_ref, gb_ref, pool_ref = refs[:8]
    refs = refs[8:]
    if stream is not None:
        kms_ref, ubuf, wbuf, pbuf, psem = refs
    else:
        ubuf, wbuf = refs
    n_lvl = max(len(POOL_WINDOWS) - 1, 1)
    t = pl.program_id(1)
    d_att = n_heads * hd
    d_model = x_ref.shape[-1]

    if stream is not None:
        gps, gpsq, total, ppb = stream
        bpg = _KM_GROUP // ppb
        step = pl.program_id(0) * nt + t
        km_rows = kms_ref.shape[1]
        lane_s = lax.broadcasted_iota(jnp.int32, (_KM_ROWS, LANES), 1)

        def group_of(k):
            gidx = step * gps + k
            half = k % 2 if gps % 2 == 0 else lax.rem(gidx, 2)
            return gidx, lax.div(gidx, gpsq), lax.rem(gidx, gpsq), half

        def page_copy(seq, p, slot, half):
            return pltpu.make_async_copy(ck_ref.at[pt_ref[seq, p]], pbuf.at[slot], psem.at[half])

        def start_group(gidx, half):
            seq, g = lax.div(gidx, gpsq), lax.rem(gidx, gpsq)
            for j in range(_KM_GROUP):
                page_copy(seq, g * _KM_GROUP + j, half * _KM_GROUP + j, half).start()

        def stream_wait(k):
            _, seq, g, half = group_of(k)
            for j in range(_KM_GROUP):
                page_copy(seq, g * _KM_GROUP + j, half * _KM_GROUP + j, half).wait()

            @pl.when(g == 0)
            def _():
                kms_ref[0] = jnp.zeros((km_rows, LANES), F32)

        def stream_reduce(k, rc):
            _, _, g, half = group_of(k)
            rs = slice(rc * _KM_ROWS, (rc + 1) * _KM_ROWS)
            upd = jnp.zeros((_KM_ROWS, LANES), F32)
            for bb in range(bpg):
                slot = half * _KM_GROUP + bb * ppb
                xs = pbuf[slot, rs, :]
                for pp in range(1, ppb):
                    xs = xs + pbuf[slot + pp, rs, :]
                upd = jnp.where(lane_s == g * bpg + bb, jnp.sum(xs, axis=-1, keepdims=True), upd)
            kms_ref[0, rs, :] = kms_ref[0, rs, :] + upd * (1.0 / MOBA_BLOCK)

        def stream_refill(k):
            gidx, _, _, half = group_of(k)

            @pl.when(gidx + 2 < total)
            def _():
                start_group(gidx + 2, half)

        @pl.when(step == 0)
        def _():
            for k0 in range(min(2, total)):
                start_group(jnp.int32(k0), k0)

        n_rc = km_rows // _KM_ROWS
        stream_work = [(k, rc) for k in range(gps) for rc in range(n_rc)]
    else:
        gps, n_rc, stream_work = 0, 0, []

    def side(n_pieces):
        for _ in range(n_pieces):
            if side.queue:
                stream_reduce(*side.queue.pop(0))
    side.queue = []

    def open_group(k):
        stream_wait(k)
        side.queue = [w for w in stream_work if w[0] == k]

    def close_group(k):
        side(len(side.queue))
        stream_refill(k)

    per = -(-n_rc // 4) if n_rc else 0

    if gps:
        open_group(0)
    x = x_ref[0]
    sh_m = mod_ref[0, 0]
    sc_m = mod_ref[0, 1]
    ms = jnp.mean(x * x, axis=-1, keepdims=True)
    h = x * lax.rsqrt(ms + EPS) * gmix_ref[...]
    h = (h * (1.0 + sc_m) + sh_m).astype(BF16)

    u = _dot(h, wu_ref[...])

    @pl.when(t == 0)
    def _():
        ubuf[0:halo, :] = st_ref[0]

    @pl.when(t > 0)
    def _():
        ubuf[0:halo, :] = ubuf[tm:tm + halo, :]

    ubuf[halo:halo + tm, :] = u
    row = lax.broadcasted_iota(jnp.int32, (tm, 1), 0) + t * tm
    gw = LANES

    def window(g):
        w = POOL_WINDOWS[g]
        cs = slice(g * gw, (g + 1) * gw)
        n_rows = halo + tm
        src, valid_from, lvl, d = None, 0, 0, 1
        while 2 * d < w:
            s = d * rows_per_tok
            a = -(-s // SUBLANES) * SUBLANES
            dst = wbuf.at[g * n_lvl + lvl]
            if src is None:
                dst[a:n_rows, :] = ubuf[a:n_rows, cs] + ubuf[a - s:n_rows - s, cs]
            else:
                dst[a:n_rows, :] = src[a:n_rows, :] + src[a - s:n_rows - s, :]
            valid_from = max(a, valid_from + s)
            src, lvl, d = dst, lvl + 1, 2 * d
        s = d * rows_per_tok
        assert halo - s >= valid_from
        if src is None:
            acc = ubuf[halo:n_rows, cs] + ubuf[halo - s:n_rows - s, cs]
        else:
            acc = src[halo:n_rows, :] + src[halo - s:n_rows - s, :]
        cnt = 1.0
        for j in range(1, w):
            first_row = (j - pos0) * rows_per_tok
            if first_row <= 0:
                cnt = cnt + 1.0
            else:
                cnt = cnt + (row >= first_row).astype(F32)
        return (acc / cnt - u[:, cs]).astype(BF16)

    kt = _dot_nt(wkt_ref[...], h)
    pooled = [window(0), window(1)]
    side(per)
    vt_ref[0] = _dot_nt(wvt_ref[...], h)
    pooled += [window(2), window(3)]
    q = _dot(h, wq_ref[...])
    kn = []
    for hh in range(n_heads):
        rs = slice(hh * hd, (hh + 1) * hd)
        kh = kt[rs]
        msk = jnp.sum(kh * kh, axis=0, keepdims=True) * (1.0 / hd)
        kn.append(kh * lax.rsqrt(msk + EPS) * gk_ref[rs])
        kt_ref[0, rs, :] = kn[hh]
    kn = jnp.concatenate(kn, axis=0)
    k2_ref[0] = jnp.transpose(kn).astype(BF16)

    @pl.when(t == 0)
    def _():
        kmt_ref[0] = jnp.zeros((d_att, LANES), F32)

    if block_means:
        lane_b = lax.broadcasted_iota(jnp.int32, (d_att, LANES), 1)
        for bb in range(tm // MOBA_BLOCK):
            part = kn[:, bb * MOBA_BLOCK:bb * MOBA_BLOCK + LANES]
            for c in range(1, MOBA_BLOCK // LANES):
                part = part + kn[:, bb * MOBA_BLOCK + c * LANES:bb * MOBA_BLOCK + (c + 1) * LANES]
            col = jnp.sum(part, axis=-1, keepdims=True) * (1.0 / MOBA_BLOCK)
            kmt_ref[0] = jnp.where(lane_b == t * (tm // MOBA_BLOCK) + bb, col, kmt_ref[0])

    ys = [_dot(pooled[g], wpg_ref[g]) for g in range(len(POOL_WINDOWS))]
    qq = q * q
    hi = qq.astype(BF16)
    lo = (qq - hi.astype(F32)).astype(BF16)
    side(per)
    parts = []
    for c in range(d_att // MXU_DIM):
        cs = slice(c * MXU_DIM, (c + 1) * MXU_DIM)
        parts.append(_dot(hi[:, cs], bd_ref[...]) + _dot(lo[:, cs], bd_ref[...]))
    side(per)
    y = jnp.concatenate(ys, axis=1) * psc_ref[...]
    a = _dot(y.astype(BF16), wbp_ref[...])
    msq = jnp.concatenate(parts, axis=1)
    q_ref[0] = (q * lax.rsqrt(msq + EPS) * gq_ref[...] * scale).astype(BF16)
    if gps:
        close_group(0)

    n_gc = 4
    gcw = 2 * d_model // n_gc
    gate = [None] * n_gc

    def finish_gate(c):
        sg = _sigmoid(gate[c])
        lo_c = c * gcw
        if lo_c < d_model:
            ag_ref[0, :, lo_c:lo_c + gcw] = (sg * a[:, lo_c:lo_c + gcw]).astype(BF16)
        else:
            gb_ref[0, :, lo_c - d_model:lo_c - d_model + gcw] = sg.astype(BF16)

    for c in range(n_gc):
        if gps > 1 and c == 0:
            open_group(1)
        gate[c] = _dot(h, wg_ref[:, c * gcw:(c + 1) * gcw])
        if c > 0:
            finish_gate(c - 1)
        side(per)
    finish_gate(n_gc - 1)
    for k in range(1, gps):
        if k > 1:
            open_group(k)
        close_group(k)

    @pl.when(t == nt - 1)
    def _():
        keep = POOL_CTX * rows_per_tok
        pool_ref[0] = ubuf[halo + tm - keep:halo + tm, :]


def _inproj_call(x, mod, state, wts, *, tm, rows_per_tok, halo, pos0, n_heads, hd, block_means, pages=None):
    n, s, d = x.shape
    nt = s // tm
    d_att = n_heads * hd
    pw = wts["w_u"].shape[1]
    keep = POOL_CTX * rows_per_tok
    stream = None
    if pages is not None:
        page_table, ck_t = pages
        n_seq, n_pages = page_table.shape
        _, km_rows, page = ck_t.shape
        gpsq = n_pages // _KM_GROUP
        total = n_seq * gpsq
        gps = total // (n * nt)
        assert n_pages % _KM_GROUP == 0 and total == gps * n * nt and gps >= 1 and km_rows % _KM_ROWS == 0
        assert gpsq % gps == 0
        stream = (gps, gpsq, total, MOBA_BLOCK // page)
    body = functools.partial(_inproj_body, tm=tm, rows_per_tok=rows_per_tok, halo=halo, pos0=pos0, nt=nt,
                             n_heads=n_heads, hd=hd, scale=hd ** -0.5 * LOG2E, block_means=block_means, stream=stream)
    rm = mod.shape[2]

    def cs(shape):
        nd = len(shape)
        return pl.BlockSpec(shape, lambda *_: (0,) * nd, pipeline_mode=pl.Buffered(1))

    def bs(shape, fn):
        return pl.BlockSpec(shape, lambda i, t, *_: fn(i, t))

    in_specs = [
        bs((1, tm, d), lambda i, t: (i, t, 0)),
        bs((1, N_MOD, rm, d), lambda i, t: (i, 0, 0, 0)),
        cs((1, d)),
        cs(wts["w_u"].shape), cs(wts["w_q"].shape), cs(wts["w_kt"].shape),
        cs(wts["w_vt"].shape), cs(wts["w_g"].shape),
        cs((1, d_att)), cs((d_att, 1)), cs((MXU_DIM, MXU_DIM)),
        cs(wts["w_pg"].shape), cs((1, pw)), cs(wts["w_bp"].shape),
        bs((1, halo, pw), lambda i, t: (i, 0, 0)),
    ]
    out_specs = [
        bs((1, d_att, tm), lambda i, t: (i, 0, t)),
        bs((1, d_att, tm), lambda i, t: (i, 0, t)),
        bs((1, tm, d_att), lambda i, t: (i, t, 0)),
        bs((1, tm, d_att), lambda i, t: (i, t, 0)),
        bs((1, d_att, LANES), lambda i, t: (i, 0, 0)),
        bs((1, tm, d), lambda i, t: (i, t, 0)),
        bs((1, tm, d), lambda i, t: (i, t, 0)),
        bs((1, keep, pw), lambda i, t: (i, 0, 0)),
    ]
    out_shape = [
        jax.ShapeDtypeStruct((n, d_att, s), F32),
        jax.ShapeDtypeStruct((n, d_att, s), F32),
        jax.ShapeDtypeStruct((n, s, d_att), BF16),
        jax.ShapeDtypeStruct((n, s, d_att), BF16),
        jax.ShapeDtypeStruct((n, d_att, LANES), F32),
        jax.ShapeDtypeStruct((n, s, d), BF16),
        jax.ShapeDtypeStruct((n, s, d), BF16),
        jax.ShapeDtypeStruct((n, keep, pw), F32),
    ]
    n_lvl = max(len(POOL_WINDOWS) - 1, 1)
    scratch = [pltpu.VMEM((halo + tm, pw), F32), pltpu.VMEM((len(POOL_WINDOWS) * n_lvl, halo + tm, LANES), F32)]
    args = [x, mod, wts["g_mix"], wts["w_u"], wts["w_q"], wts["w_kt"], wts["w_vt"], wts["w_g"], wts["g_q"], wts["g_k"],
            wts["bd"], wts["w_pg"], wts["p_scale"], wts["w_bp"], state]
    n_prefetch = 0
    if stream is not None:
        gps, gpsq = stream[0], stream[1]
        in_specs.append(pl.BlockSpec(memory_space=pl.ANY))
        out_specs.append(bs((1, km_rows, LANES), lambda i, t: ((i * nt + t) * gps // gpsq, 0, 0)))
        out_shape.append(jax.ShapeDtypeStruct((n_seq, km_rows, LANES), F32))
        scratch += [pltpu.VMEM((2 * _KM_GROUP, km_rows, page), F32), pltpu.SemaphoreType.DMA((2,))]
        args = [page_table] + args + [ck_t]
        n_prefetch = 1
    grid_spec = pltpu.PrefetchScalarGridSpec(num_scalar_prefetch=n_prefetch, grid=(n, nt), in_specs=in_specs,
                                             out_specs=out_specs, scratch_shapes=scratch)
    return pl.pallas_call(
        body, grid_spec=grid_spec, out_shape=out_shape,
        compiler_params=pltpu.CompilerParams(dimension_semantics=("arbitrary", "arbitrary"),
                                             vmem_limit_bytes=VMEM_LIMIT),
        name="mixer_inproj",
    )(*args)


def _attn_prompt_body(q_ref, k2_ref, kmt_ref, vt_ref, o_ref, va_ref, qh_ref, pen_ref, s_ref, p_ref,
                      *, seq, nb, hd):
    blk = MOBA_BLOCK
    nbp = SUBLANES
    q2 = q_ref[0]
    ones_rows = (lax.broadcasted_iota(jnp.int32, (SUBLANES, seq), 0) == 0).astype(F32)
    for hh in range(2):
        va = jnp.concatenate([vt_ref[0, hh * hd:(hh + 1) * hd, :], ones_rows, jnp.zeros((SUBLANES, seq), F32)], axis=0)
        va_ref[hh] = va.astype(BF16)
    km = jnp.transpose(kmt_ref[0])
    lane_q = lax.broadcasted_iota(jnp.int32, (seq, 2 * hd), 1)
    lane_m = lax.broadcasted_iota(jnp.int32, (2 * SUBLANES, 2 * hd), 1)
    tpos = lax.broadcasted_iota(jnp.int32, (nbp, seq), 1)
    bidx = lax.broadcasted_iota(jnp.int32, (nbp, seq), 0)
    qblk = jnp.zeros((nbp, seq), jnp.int32)
    for b in range(1, nb):
        qblk = qblk + (tpos >= b * blk).astype(jnp.int32)
    for hh in range(2):
        in_head = (lane_m >= hh * hd) & (lane_m < (hh + 1) * hd)
        kmh = jnp.where(in_head, km[0:2 * SUBLANES], 0.0).astype(BF16)
        sct = _dot_nt(kmh, q2)[0:nbp]
        rank = jnp.zeros((nbp, seq), F32)
        for bp in range(nb):
            rowv = sct[bp:bp + 1, :]
            beats = (rowv > sct) | ((rowv == sct) & (bp < bidx))
            rank = rank + jnp.where(beats & (bp < qblk), 1.0, 0.0)
        keep = (bidx < qblk) & (rank < MOBA_TOPK)
        pen_ref[hh] = jnp.where(keep, 0.0, NEG)
        own_lanes = (lane_q >= hh * hd) & (lane_q < (hh + 1) * hd)
        qh_ref[hh] = jnp.where(own_lanes, q2, jnp.zeros_like(q2))

    kr = lax.broadcasted_iota(jnp.int32, (blk, blk), 0)
    qc = lax.broadcasted_iota(jnp.int32, (blk, blk), 1)
    causal = kr <= qc
    items = [(i, hh) for i in range(nb) for hh in range(2)]
    m_of, zero_of, o_of = {}, {}, {}

    def base_of(i):
        return blk * (i * (i + 1) // 2)

    def logits_piece(k, j):
        i, hh = items[k]
        qs = slice(i * blk, (i + 1) * blk)
        sv = _dot_nt(k2_ref[0, j * blk:(j + 1) * blk, :], qh_ref[hh, qs, :])
        s_ref[hh, base_of(i) + j * blk:base_of(i) + (j + 1) * blk, :] = sv
        if j == i:
            mj = jnp.max(jnp.where(causal, sv, NEG), axis=0, keepdims=True)
        else:
            mj = jnp.max(sv, axis=0, keepdims=True) + pen_ref[hh, j:j + 1, qs]
        m_of[k] = mj if j == 0 else jnp.maximum(m_of[k], mj)

    def exp_piece(k, j):
        i, hh = items[k]
        qs = slice(i * blk, (i + 1) * blk)
        if j == 0:
            zero_of[k] = [jnp.zeros((1, blk), F32)] * _EXP_DEPTH
        shift = m_of[k] if j == i else m_of[k] - pen_ref[hh, j:j + 1, qs]
        for c in range(blk // _EXP_ROWS):
            r0 = base_of(i) + j * blk + c * _EXP_ROWS
            x = s_ref[hh, r0:r0 + _EXP_ROWS, :]
            if j == i:
                x = jnp.where(causal[c * _EXP_ROWS:(c + 1) * _EXP_ROWS], x, NEG)
            p = jnp.exp2(x - (shift + zero_of[k][c % _EXP_DEPTH]))
            p_ref[hh, r0:r0 + _EXP_ROWS, :] = p.astype(BF16)
            zero_of[k][c % _EXP_DEPTH] = jnp.minimum(p[_EXP_ROWS - 1:_EXP_ROWS], 0.0)

    def values(k):
        i, hh = items[k]
        ln = (i + 1) * blk
        ot = _dot(va_ref[hh, :, 0:ln], p_ref[hh, base_of(i):base_of(i) + ln, :])
        o_of[k] = ot[0:hd] / ot[hd:hd + 1]
        if hh == 1:
            both = jnp.concatenate([o_of.pop(k - 1), o_of.pop(k)], axis=0)
            o_ref[0, i * blk:(i + 1) * blk, :] = jnp.transpose(both).astype(BF16)

    for j in range(items[0][0] + 1):
        logits_piece(0, j)
    for k in range(len(items)):
        ahead = [(k + 1, j) for j in range(items[k + 1][0] + 1)] if k + 1 < len(items) else []
        for j in range(items[k][0] + 1):
            exp_piece(k, j)
            if ahead:
                logits_piece(*ahead.pop(0))
        for kj in ahead:
            logits_piece(*kj)
        values(k)


def _attn_prompt_call(q, k2, kmt, vt, *, n_heads, hd):
    n, s, d_att = q.shape
    nb = s // MOBA_BLOCK
    body = functools.partial(_attn_prompt_body, seq=s, nb=nb, hd=hd)
    tri = MOBA_BLOCK * (nb * (nb + 1) // 2)
    tok_spec = pl.BlockSpec((1, s, 2 * hd), lambda i, p: (i, 0, p))
    return pl.pallas_call(
        body, grid=(n, n_heads // 2),
        in_specs=[tok_spec, tok_spec,
                  pl.BlockSpec((1, 2 * hd, LANES), lambda i, p: (i, p, 0)),
                  pl.BlockSpec((1, 2 * hd, s), lambda i, p: (i, p, 0))],
        out_specs=tok_spec,
        out_shape=jax.ShapeDtypeStruct((n, s, d_att), BF16),
        scratch_shapes=[pltpu.VMEM((2, hd + 2 * SUBLANES, s), BF16),
                        pltpu.VMEM((2, s, 2 * hd), BF16), pltpu.VMEM((2, SUBLANES, s), F32),
                        pltpu.VMEM((2, tri, MOBA_BLOCK), F32), pltpu.VMEM((2, tri, MOBA_BLOCK), BF16)],
        compiler_params=pltpu.CompilerParams(dimension_semantics=("arbitrary", "arbitrary"),
                                             vmem_limit_bytes=VMEM_LIMIT),
        name="moba_prompt",
    )(q, k2, kmt, vt)


def _post_body(x_ref, o_ref, ag_ref, gb_ref, mod_ref, wba_ref, wout_ref, gffn_ref, wup_ref, wconv_ref, bconv_ref,
               wdown_ref, cst_ref, y_ref, conv_ref, gbuf, act_ref, *, tm, rows_per_tok, halo, nt, pc, down_groups, d_ff):
    t = pl.program_id(1)
    x = x_ref[0]
    gt_m = mod_ref[0, 2]
    sh_f = mod_ref[0, 3]
    sc_f = mod_ref[0, 4]
    gt_f = mod_ref[0, 5]
    oa = _dot(o_ref[0], wba_ref[...])
    merged = ag_ref[0].astype(F32) + gb_ref[0].astype(F32) * oa
    x1 = x + gt_m * _dot(merged.astype(BF16), wout_ref[...])
    ms = jnp.mean(x1 * x1, axis=-1, keepdims=True)
    h2 = x1 * lax.rsqrt(ms + EPS) * gffn_ref[...]
    h2 = (h2 * (1.0 + sc_f) + sh_f).astype(BF16)

    @pl.when(t == 0)
    def _():
        gbuf[0:halo, :] = cst_ref[0]

    @pl.when(t > 0)
    def _():
        gbuf[0:halo, :] = gbuf[tm:tm + halo, :]

    r = rows_per_tok
    n_pc = d_ff // pc

    def up(c):
        cs = slice(c * pc, (c + 1) * pc)
        fg = _dot(h2, wup_ref[:, cs])
        fv = _dot(h2, wup_ref[:, d_ff + c * pc:d_ff + (c + 1) * pc])
        gbuf[halo:halo + tm, cs] = fg
        return fg, fv

    def activate(c, fg, fv):
        cs = slice(c * pc, (c + 1) * pc)
        conv = bconv_ref[:, cs]
        conv = conv + gbuf[halo - 2 * r:halo - 2 * r + tm, cs] * wconv_ref[0:1, cs]
        conv = conv + gbuf[halo - r:halo - r + tm, cs] * wconv_ref[1:2, cs]
        conv = conv + fg * wconv_ref[2:3, cs]
        act_ref[:, cs] = (conv * _sigmoid(conv) * fv).astype(BF16)

    bounds = [round(g * n_pc / down_groups) * pc for g in range(down_groups + 1)]
    acc = None
    nxt = up(0)
    for c in range(n_pc):
        cur = nxt
        if c + 1 < n_pc:
            nxt = up(c + 1)
        activate(c, *cur)
        if (c + 1) * pc in bounds[1:]:
            lo = bounds[bounds.index((c + 1) * pc) - 1]
            part = _dot(act_ref[:, lo:(c + 1) * pc], wdown_ref[lo:(c + 1) * pc, :])
            acc = part if acc is None else acc + part
    y_ref[0] = x1 + gt_f * acc

    @pl.when(t == nt - 1)
    def _():
        keep = (CONV_W - 1) * r
        conv_ref[0] = gbuf[halo + tm - keep:halo + tm, :]


def _post_call(x, o, ag, gb, mod, cstate, wts, *, tm, rows_per_tok, halo, pc, down_groups):
    n, s, d = x.shape
    nt = s // tm
    d_att = o.shape[-1]
    d_ff = wts["w_down"].shape[0]
    keep = (CONV_W - 1) * rows_per_tok
    rm = mod.shape[2]
    body = functools.partial(_post_body, tm=tm, rows_per_tok=rows_per_tok, halo=halo, nt=nt, pc=pc,
                             down_groups=down_groups, d_ff=d_ff)
    row_spec = lambda w: pl.BlockSpec((1, tm, w), lambda i, t: (i, t, 0))
    in_specs = [
        row_spec(d), row_spec(d_att), row_spec(d), row_spec(d),
        pl.BlockSpec((1, N_MOD, rm, d), lambda i, t: (i, 0, 0, 0)),
        _const_spec(wts["w_ba"].shape), _const_spec(wts["w_out"].shape), _const_spec((1, d)),
        _const_spec(wts["w_up"].shape), _const_spec((CONV_W, d_ff)), _const_spec((1, d_ff)),
        _const_spec(wts["w_down"].shape),
        pl.BlockSpec((1, halo, d_ff), lambda i, t: (i, 0, 0)),
    ]
    out_specs = [row_spec(d), pl.BlockSpec((1, keep, d_ff), lambda i, t: (i, 0, 0))]
    out_shape = [jax.ShapeDtypeStruct((n, s, d), F32), jax.ShapeDtypeStruct((n, keep, d_ff), F32)]
    return pl.pallas_call(
        body, grid=(n, nt), in_specs=in_specs, out_specs=out_specs, out_shape=out_shape,
        scratch_shapes=[pltpu.VMEM((halo + tm, d_ff), F32), pltpu.VMEM((tm, d_ff), BF16)],
        compiler_params=pltpu.CompilerParams(dimension_semantics=("arbitrary", "arbitrary"),
                                             vmem_limit_bytes=VMEM_LIMIT),
        name="merge_ffn",
    )(x, o, ag, gb, mod, wts["w_ba"], wts["w_out"], wts["g_ffn"], wts["w_up"], wts["w_conv"], wts["b_conv"],
      wts["w_down"], cstate)


def _kmean_body(pt_ref, ck_ref, km_ref, buf, sem, *, n_seq, n_pages, ppb):
    n = pl.program_id(0)
    rows = km_ref.shape[1]
    bpg = _KM_GROUP // ppb
    n_groups = n_pages // _KM_GROUP

    def copy(seq, p, slot):
        return pltpu.make_async_copy(ck_ref.at[pt_ref[seq, p]], buf.at[slot], sem.at[slot])

    def start_group(seq, g, half):
        for k in range(_KM_GROUP):
            copy(seq, g * _KM_GROUP + k, half * _KM_GROUP + k).start()

    @pl.when(n == 0)
    def _():
        start_group(0, 0, 0)
        start_group(0, 1, 1)

    km_ref[0] = jnp.zeros((rows, LANES), F32)
    lane = lax.broadcasted_iota(jnp.int32, (_KM_ROWS, LANES), 1)

    def pair(g2, carry):
        for half in range(2):
            g = g2 * 2 + half
            for k in range(_KM_GROUP):
                copy(n, g * _KM_GROUP + k, half * _KM_GROUP + k).wait()
            for rc in range(rows // _KM_ROWS):
                rs = slice(rc * _KM_ROWS, (rc + 1) * _KM_ROWS)
                upd = jnp.zeros((_KM_ROWS, LANES), F32)
                for bb in range(bpg):
                    slot = half * _KM_GROUP + bb * ppb
                    x = buf[slot, rs, :]
                    for pp in range(1, ppb):
                        x = x + buf[slot + pp, rs, :]
                    col = jnp.sum(x, axis=-1, keepdims=True)
                    upd = jnp.where(lane == g * bpg + bb, col, upd)
                km_ref[0, rs, :] = km_ref[0, rs, :] + upd * (1.0 / MOBA_BLOCK)
            nxt = g + 2

            @pl.when(nxt < n_groups)
            def _():
                start_group(n, nxt, half)

            @pl.when((nxt >= n_groups) & (n + 1 < n_seq))
            def _():
                start_group(n + 1, nxt - n_groups, half)
        return carry

    lax.fori_loop(0, n_groups // 2, pair, 0)


def _kmean_call(page_table, ck_t):
    n, n_pages = page_table.shape
    _, rows, page = ck_t.shape
    ppb = MOBA_BLOCK // page
    assert n_pages % (2 * _KM_GROUP) == 0 and rows % _KM_ROWS == 0
    body = functools.partial(_kmean_body, n_seq=n, n_pages=n_pages, ppb=ppb)
    grid_spec = pltpu.PrefetchScalarGridSpec(
        num_scalar_prefetch=1, grid=(n,),
        in_specs=[pl.BlockSpec(memory_space=pl.ANY)],
        out_specs=pl.BlockSpec((1, rows, LANES), lambda i, pt: (i, 0, 0)),
        scratch_shapes=[pltpu.VMEM((2 * _KM_GROUP, rows, page), F32), pltpu.SemaphoreType.DMA((2 * _KM_GROUP,))],
    )
    return pl.pallas_call(
        body, grid_spec=grid_spec,
        out_shape=jax.ShapeDtypeStruct((n, rows, LANES), F32),
        compiler_params=pltpu.CompilerParams(dimension_semantics=("arbitrary",), vmem_limit_bytes=VMEM_LIMIT),
        name="cache_block_means",
    )(page_table, ck_t)


def _topk_body(q_ref, km_ref, sel_ref, *, n_heads, hd, n_blk, top):
    n_seq = q_ref.shape[0]
    rows = n_heads * SUBLANES
    lane_q = lax.broadcasted_iota(jnp.int32, (SUBLANES, n_heads * hd), 1)
    lane = lax.broadcasted_iota(jnp.int32, (rows, LANES), 1).astype(F32)

    def per_seq(i, carry):
        q8 = q_ref[i]
        qbd = jnp.concatenate(
            [jnp.where((lane_q >= hh * hd) & (lane_q < (hh + 1) * hd), q8, 0.0) for hh in range(n_heads)], axis=0)
        sc = _dot(qbd.astype(BF16), km_ref[i].astype(BF16))
        sc = jnp.where(lane < n_blk, sc, -jnp.inf)
        out = jnp.zeros((rows, LANES), F32)
        for r in range(top):
            m = jnp.max(sc, axis=-1, keepdims=True)
            idx = jnp.min(jnp.where(sc == m, lane, float(LANES)), axis=-1, keepdims=True)
            out = jnp.where(lane == r, idx, out)
            sc = jnp.where(lane == idx, -jnp.inf, sc)
        sel_ref[i] = out.astype(jnp.int32)
        return carry

    lax.fori_loop(0, n_seq, per_seq, 0)


def _topk_call(q8, kmt, *, n_heads, hd, n_blk, top):
    n = q8.shape[0]
    rows = n_heads * SUBLANES
    body = functools.partial(_topk_body, n_heads=n_heads, hd=hd, n_blk=n_blk, top=top)
    return pl.pallas_call(
        body, grid=(1,),
        in_specs=[pl.BlockSpec(q8.shape, lambda i: (0, 0, 0)), pl.BlockSpec(kmt.shape, lambda i: (0, 0, 0))],
        out_specs=pl.BlockSpec((n, rows, LANES), lambda i: (0, 0, 0)),
        out_shape=jax.ShapeDtypeStruct((n, rows, LANES), jnp.int32),
        compiler_params=pltpu.CompilerParams(vmem_limit_bytes=VMEM_LIMIT),
        name="block_topk",
    )(q8, kmt)


def _attn_sample_body(pt_ref, sel_ref, qt_ref, knt_ref, vnt_ref, ck_ref, cv_ref, o_ref, kbuf, vbuf, lg_ref, p_ref, sem,
                      *, n_seq, n_heads, hd, n_tok, top, ppb):
    n = pl.program_id(0)
    n_sel = top * ppb
    n_grp = n_heads * n_tok
    slot = lax.rem(n, 2)

    def gather(seq, sl):
        for hh in range(n_heads):
            for s in range(n_tok):
                g = hh * n_tok + s
                for j in range(n_sel):
                    r, pp = divmod(j, ppb)
                    page = pt_ref[seq, sel_ref[seq, g * top + r] * ppb + pp]
                    pltpu.make_async_copy(ck_ref.at[page, hh], kbuf.at[sl, g, j], sem.at[sl]).start()
                    pltpu.make_async_copy(cv_ref.at[page, hh], vbuf.at[sl, g, j], sem.at[sl]).start()

    @pl.when(n == 0)
    def _():
        gather(0, 0)

    @pl.when(n + 1 < n_seq)
    def _():
        gather(n + 1, 1 - slot)

    for g in range(n_grp):
        for j in range(n_sel):
            pltpu.make_async_copy(ck_ref.at[0, 0], kbuf.at[slot, g, j], sem.at[slot]).wait()
            pltpu.make_async_copy(cv_ref.at[0, 0], vbuf.at[slot, g, j], sem.at[slot]).wait()

    lane = lax.broadcasted_iota(jnp.int32, (1, LANES), 1)
    pad_rows = [jnp.full((1, LANES), NEG, F32)] * (SUBLANES - n_sel - 1)
    for hh in range(n_heads):
        rs = slice(hh * hd, (hh + 1) * hd)
        knt = knt_ref[0, rs, :]
        for s in range(n_tok):
            g = hh * n_tok + s
            qb = jnp.broadcast_to(qt_ref[0, rs, s:s + 1], (hd, LANES))
            rows = [jnp.sum(kbuf[slot, g, j] * qb, axis=0, keepdims=True) for j in range(n_sel)]
            own = jnp.sum(knt * qb, axis=0, keepdims=True)
            rows.append(jnp.where(lane <= s, own, NEG))
            lg_ref[g] = jnp.concatenate(rows + pad_rows, axis=0)

    lg = lg_ref[...]
    m = jnp.max(jnp.max(lg, axis=1), axis=-1, keepdims=True)
    mb = jnp.broadcast_to(m, (n_grp, LANES))
    for g in range(n_grp):
        p_ref[g] = jnp.exp2(lg_ref[g] - mb[g:g + 1, :])
    l = jnp.sum(jnp.sum(p_ref[...], axis=1), axis=-1, keepdims=True)
    inv = jnp.broadcast_to(1.0 / l, (n_grp, LANES))

    lane_o = lax.broadcasted_iota(jnp.int32, (hd, LANES), 1)
    for hh in range(n_heads):
        rs = slice(hh * hd, (hh + 1) * hd)
        vnt = vnt_ref[0, rs, :]
        out_h = jnp.zeros((hd, LANES), F32)
        for s in range(n_tok):
            g = hh * n_tok + s
            pg = p_ref[g] * inv[g:g + 1, :]
            acc = vnt * pg[n_sel:n_sel + 1, :]
            for j in range(n_sel):
                acc = acc + vbuf[slot, g, j] * pg[j:j + 1, :]
            col = jnp.sum(acc, axis=-1, keepdims=True)
            out_h = jnp.where(lane_o == s, col, out_h)
        o_ref[0, rs, :] = out_h


def _attn_sample_call(page_table, sel, qt, knt, vnt, ck_t, cv_t, *, n_heads, hd, n_tok, top):
    n = page_table.shape[0]
    page = ck_t.shape[-1]
    ppb = MOBA_BLOCK // page
    d_att = n_heads * hd
    n_grp = n_heads * n_tok
    assert top * ppb + 1 <= SUBLANES
    body = functools.partial(_attn_sample_body, n_seq=n, n_heads=n_heads, hd=hd, n_tok=n_tok, top=top, ppb=ppb)
    blk = pl.BlockSpec((1, d_att, LANES), lambda i, pt, sl: (i, 0, 0))
    grid_spec = pltpu.PrefetchScalarGridSpec(
        num_scalar_prefetch=2, grid=(n,),
        in_specs=[blk, blk, blk, pl.BlockSpec(memory_space=pl.ANY), pl.BlockSpec(memory_space=pl.ANY)],
        out_specs=blk,
        scratch_shapes=[pltpu.VMEM((2, n_grp, top * ppb, hd, page), F32),
                        pltpu.VMEM((2, n_grp, top * ppb, hd, page), F32),
                        pltpu.VMEM((n_grp, SUBLANES, LANES), F32),
                        pltpu.VMEM((n_grp, SUBLANES, LANES), F32),
                        pltpu.SemaphoreType.DMA((2,))],
    )
    return pl.pallas_call(
        body, grid_spec=grid_spec,
        out_shape=jax.ShapeDtypeStruct((n, d_att, LANES), F32),
        compiler_params=pltpu.CompilerParams(dimension_semantics=("arbitrary",), vmem_limit_bytes=VMEM_LIMIT),
        name="moba_sample",
    )(page_table, sel, qt, knt, vnt, ck_t, cv_t)


def _layer_weights(l, g_norm_mix, w_in, g_q, g_k, w_pool_group, pool_scale, w_branch_pool, w_branch_attn, w_out,
                   g_norm_ffn, w_up, w_conv, b_conv, w_down, n_heads, hd):
    d = w_in.shape[1]
    pw = w_pool_group.shape[1] * w_pool_group.shape[2]
    d_att = n_heads * hd
    wi = w_in[l].astype(BF16)
    c0, c1, c2, c3 = pw, pw + d_att, pw + 2 * d_att, pw + 3 * d_att
    blk = jnp.arange(MXU_DIM) // hd
    return {
        "g_mix": g_norm_mix[l].reshape(1, d),
        "w_u": wi[:, :c0], "w_q": wi[:, c0:c1], "w_kt": wi[:, c1:c2].T, "w_vt": wi[:, c2:c3].T, "w_g": wi[:, c3:],
        "g_q": g_q[l].reshape(1, d_att), "g_k": g_k[l].reshape(d_att, 1),
        "bd": jnp.where(blk[:, None] == blk[None, :], 1.0 / hd, 0.0).astype(BF16),
        "w_pg": w_pool_group[l].astype(BF16), "p_scale": pool_scale[l].reshape(1, pw),
        "w_bp": w_branch_pool[l].astype(BF16),
        "w_ba": w_branch_attn[l].astype(BF16), "w_out": w_out[l].astype(BF16),
        "g_ffn": g_norm_ffn[l].reshape(1, d), "w_up": w_up[l].astype(BF16),
        "w_conv": w_conv[l], "b_conv": b_conv[l].reshape(1, -1), "w_down": w_down[l].astype(BF16),
    }


def kernel(x_prompt, x_sample, cache_k, cache_v, state_pool, state_ffn_conv, page_table, c_prompt, c_sample, w_ada, b_ada, g_norm_mix, w_in, g_q, g_k, w_pool_group, pool_scale, w_branch_pool, w_branch_attn, w_out, g_norm_ffn, w_up, w_conv, b_conv, w_down):
    nb_p, seq, d = x_prompt.shape
    nb_s, n_tok, _ = x_sample.shape
    depth, n_phys, page, n_heads, hd = cache_k.shape
    d_att = n_heads * hd
    pw = state_pool.shape[-1]
    d_ff = w_down.shape[1]
    n_pages = page_table.shape[1]
    past = n_pages * page
    assert MOBA_BLOCK % page == 0 and past % MOBA_BLOCK == 0 and seq % MOBA_BLOCK == 0
    assert n_tok <= SUBLANES
    n_blk = past // MOBA_BLOCK
    top = min(MOBA_TOPK, n_blk)
    assert 0 < n_blk <= LANES and seq // MOBA_BLOCK <= SUBLANES
    tm = 256
    pc, dg = MXU_DIM, 3
    assert d_ff % pc == 0
    rows_s = nb_s * n_tok

    y_p, y_s = x_prompt, x_sample.transpose(1, 0, 2).reshape(1, rows_s, d)
    halo_p = 4 * SUBLANES
    zero_pool = jnp.zeros((nb_p, halo_p, pw), F32)
    zero_conv = jnp.zeros((nb_p, SUBLANES, d_ff), F32)
    outs = [[] for _ in range(8)]
    for l in range(depth):
        wts = _layer_weights(l, g_norm_mix, w_in, g_q, g_k, w_pool_group, pool_scale, w_branch_pool, w_branch_attn,
                             w_out, g_norm_ffn, w_up, w_conv, b_conv, w_down, n_heads, hd)
        mod = _ada_call(jnp.concatenate([c_prompt, c_sample], axis=0), w_ada[l].astype(BF16), b_ada[l])
        mod_p = mod[:nb_p].reshape(nb_p, N_MOD, 1, d)
        mod_s = jnp.tile(mod[nb_p:].reshape(nb_s, N_MOD, d), (n_tok, 1, 1)).transpose(1, 0, 2)[None]

        ck_t = cache_k[l].transpose(0, 2, 3, 1)
        cv_t = cache_v[l].transpose(0, 2, 3, 1)
        ck_flat = ck_t.reshape(n_phys, d_att, page)
        gpsq, steps = n_pages // _KM_GROUP, nb_p * (seq // tm)
        gps = nb_s * gpsq // steps
        fuse = n_pages % _KM_GROUP == 0 and gps >= 1 and gps * steps == nb_s * gpsq and gpsq % gps == 0
        res = _inproj_call(y_p, mod_p, zero_pool, wts, tm=tm, rows_per_tok=1, halo=halo_p, pos0=0,
                           n_heads=n_heads, hd=hd, block_means=True, pages=(page_table, ck_flat) if fuse else None)
        kt, vt, q, k2, kmt, ag, gb, pool_p = res[:8]
        kmt_s = res[8] if fuse else _kmean_call(page_table, ck_flat)
        o = _attn_prompt_call(q, k2, kmt, vt, n_heads=n_heads, hd=hd)
        y_p, conv_p = _post_call(y_p, o, ag, gb, mod_p, zero_conv, wts, tm=tm, rows_per_tok=1, halo=SUBLANES, pc=pc,
                                 down_groups=dg)
        to_rows = lambda a: a.reshape(nb_p, n_heads, hd, seq).transpose(0, 3, 1, 2)
        outs[0].append(to_rows(kt)); outs[1].append(to_rows(vt)); outs[2].append(pool_p); outs[3].append(conv_p)

        pool_state = state_pool[l].transpose(1, 0, 2).reshape(1, POOL_CTX * nb_s, pw)
        conv_state = state_ffn_conv[l].transpose(1, 0, 2).reshape(1, (CONV_W - 1) * nb_s, d_ff)
        kt_s, vt_s, q_s, _, _, ag_s, gb_s, pool_s = _inproj_call(
            y_s, mod_s, pool_state, wts, tm=rows_s, rows_per_tok=nb_s, halo=POOL_CTX * nb_s, pos0=past,
            n_heads=n_heads, hd=hd, block_means=False)
        q_ns = q_s.reshape(n_tok, nb_s, d_att).transpose(1, 0, 2).astype(F32)
        q8 = jnp.pad(q_ns, ((0, 0), (0, SUBLANES - n_tok), (0, 0)))
        sel = _topk_call(q8, kmt_s, n_heads=n_heads, hd=hd, n_blk=n_blk, top=top)
        sel = sel.reshape(nb_s, n_heads, SUBLANES, LANES)[:, :, :n_tok, :top].reshape(nb_s, n_heads * n_tok * top)
        lane_pad = lambda a: jnp.pad(a.reshape(d_att, n_tok, nb_s).transpose(2, 0, 1),
                                     ((0, 0), (0, 0), (0, LANES - n_tok)))
        qt = jnp.pad(q_ns.transpose(0, 2, 1), ((0, 0), (0, 0), (0, LANES - n_tok)))
        ot = _attn_sample_call(page_table, sel, qt, lane_pad(kt_s[0]), lane_pad(vt_s[0]), ck_t, cv_t,
                               n_heads=n_heads, hd=hd, n_tok=n_tok, top=top)
        o_s = ot[:, :, :n_tok].transpose(2, 0, 1).reshape(1, rows_s, d_att).astype(BF16)
        y_s, conv_s = _post_call(y_s, o_s, ag_s, gb_s, mod_s, conv_state, wts, tm=rows_s, rows_per_tok=nb_s,
                                 halo=(CONV_W - 1) * nb_s, pc=pc, down_groups=dg)
        to_rows_s = lambda a: a[0].reshape(n_heads, hd, n_tok, nb_s).transpose(3, 2, 0, 1)
        outs[4].append(to_rows_s(kt_s)); outs[5].append(to_rows_s(vt_s))
        outs[6].append(pool_s.reshape(POOL_CTX, nb_s, pw).transpose(1, 0, 2))
        outs[7].append(conv_s.reshape(CONV_W - 1, nb_s, d_ff).transpose(1, 0, 2))

    y_s_out = y_s.reshape(n_tok, nb_s, d).transpose(1, 0, 2)
    st = [jnp.stack(v) for v in outs]
    return (y_p, y_s_out, st[0], st[1], st[2], st[3], st[4], st[5], st[6], st[7])
```

```python
import functools

import jax
import jax.numpy as jnp
from jax import lax
from jax.experimental import pallas as pl
from jax.experimental.pallas import tpu as pltpu

F32 = jnp.float32
BF16 = jnp.bfloat16

MOBA_BLOCK = 256
MOBA_TOPK = 3
POOL_WINDOWS = (2, 4, 8, 16)
POOL_CTX = max(POOL_WINDOWS) - 1
CONV_W = 3
N_MOD = 6
EPS = 1e-6
NEG = -1e30
LOG2E = 1.4426950408889634

LANES = 128
SUBLANES = 8
MXU_DIM = 256
VMEM_LIMIT = 60 * 1024 * 1024
_EXP_ROWS = 64
_EXP_DEPTH = 1
_KM_GROUP = 16
_KM_ROWS = 64


def _const_spec(shape):
    nd = len(shape)
    return pl.BlockSpec(shape, lambda *_: (0,) * nd, pipeline_mode=pl.Buffered(1))


def _dot(a, b):
    return jnp.dot(a, b, preferred_element_type=F32)


def _dot_nt(a, b):
    return lax.dot_general(a, b, (((1,), (1,)), ((), ())), preferred_element_type=F32)


def _sigmoid(x):
    return 1.0 / (1.0 + jnp.exp(-x))


def _ada_body(c_ref, w_ref, b_ref, o_ref):
    c = c_ref[...]
    a = (c * _sigmoid(c)).astype(BF16)
    o_ref[...] = _dot(a, w_ref[...]) + b_ref[...]


def _ada_call(c, w_ada, b_ada):
    n, d = c.shape
    dm = w_ada.shape[1]
    bn = d
    return pl.pallas_call(
        _ada_body,
        grid=(dm // bn,),
        in_specs=[pl.BlockSpec((n, d), lambda j: (0, 0)),
                  pl.BlockSpec((d, bn), lambda j: (0, j)),
                  pl.BlockSpec((1, bn), lambda j: (0, j))],
        out_specs=pl.BlockSpec((n, bn), lambda j: (0, j)),
        out_shape=jax.ShapeDtypeStruct((n, dm), F32),
        name="ada_mod",
    )(c, w_ada, b_ada.reshape(1, dm))


def _inproj_body(*refs, tm, rows_per_tok, halo, pos0, nt, n_heads, hd, scale, block_means, stream):
    if stream is not None:
        pt_ref, refs = refs[0], refs[1:]
    (x_ref, mod_ref, gmix_ref, wu_ref, wq_ref, wkt_ref, wvt_ref, wg_ref, gq_ref, gk_ref, bd_ref, wpg_ref, psc_ref,
     wbp_ref, st_ref) = refs[:15]
    refs = refs[15:]
    if stream is not None:
        ck_ref, refs = refs[0], refs[1:]
    kt_ref, vt_ref, q_ref, k2_ref, kmt_ref, ag_ref, gb_ref, pool_ref = refs[:8]
    refs = refs[8:]
    if stream is not None:
        kms_ref, ubuf, wbuf, pbuf, psem = refs
    else:
        ubuf, wbuf = refs
    n_lvl = max(len(POOL_WINDOWS) - 1, 1)
    t = pl.program_id(1)
    d_att = n_heads * hd
    d_model = x_ref.shape[-1]
    cw = MXU_DIM

    todo = []
    if stream is not None:
        gps, gpsq, total, ppb = stream
        bpg = _KM_GROUP // ppb
        step = pl.program_id(0) * nt + t
        n_steps = total // gps
        km_rows = kms_ref.shape[1]
        lane_s = lax.broadcasted_iota(jnp.int32, (_KM_ROWS, LANES), 1)
        side_set = lax.rem(step, 2)

        def batch_copies(b, sset):
            out = []
            for k in range(gps):
                gidx = b * gps + k
                seq, g = lax.div(gidx, gpsq), lax.rem(gidx, gpsq)
                for j in range(_KM_GROUP):
                    slot = (sset * gps + k) * _KM_GROUP + j
                    out.append(pltpu.make_async_copy(ck_ref.at[pt_ref[seq, g * _KM_GROUP + j]], pbuf.at[slot],
                                                     psem.at[sset]))
            return out

        @pl.when(step == 0)
        def _():
            for b0 in range(min(2, n_steps)):
                for cp in batch_copies(jnp.int32(b0), b0):
                    cp.start()

        for slot in range(gps * _KM_GROUP):
            pltpu.make_async_copy(ck_ref.at[0], pbuf.at[side_set * gps * _KM_GROUP + slot], psem.at[side_set]).wait()

        @pl.when(lax.rem(step * gps, gpsq) == 0)
        def _():
            kms_ref[0] = jnp.zeros((km_rows, LANES), F32)

        def stream_reduce(k, rc):
            g = lax.rem(step * gps + k, gpsq)
            rs = slice(rc * _KM_ROWS, (rc + 1) * _KM_ROWS)
            upd = jnp.zeros((_KM_ROWS, LANES), F32)
            for bb in range(bpg):
                slot = (side_set * gps + k) * _KM_GROUP + bb * ppb
                xs = pbuf[slot, rs, :]
                for pp in range(1, ppb):
                    xs = xs + pbuf[slot + pp, rs, :]
                upd = jnp.where(lane_s == g * bpg + bb, jnp.sum(xs, axis=-1, keepdims=True), upd)
            kms_ref[0, rs, :] = kms_ref[0, rs, :] + upd * (1.0 / MOBA_BLOCK)

        todo = [(k, rc) for k in range(gps) for rc in range(km_rows // _KM_ROWS)]

    def side(n_pieces=1):
        for _ in range(n_pieces):
            if todo:
                stream_reduce(*todo.pop(0))

    x = x_ref[0]
    sh_m = mod_ref[0, 0]
    sc_m = mod_ref[0, 1]
    ms = jnp.mean(x * x, axis=-1, keepdims=True)
    h = x * lax.rsqrt(ms + EPS) * gmix_ref[...]
    h = (h * (1.0 + sc_m) + sh_m).astype(BF16)

    pw = wu_ref.shape[1]
    u = jnp.concatenate([_dot(h, wu_ref[:, c * cw:(c + 1) * cw]) for c in range(pw // cw)], axis=1)

    @pl.when(t == 0)
    def _():
        ubuf[0:halo, :] = st_ref[0]

    @pl.when(t > 0)
    def _():
        ubuf[0:halo, :] = ubuf[tm:tm + halo, :]

    ubuf[halo:halo + tm, :] = u
    gw = LANES

    def window(g):
        w = POOL_WINDOWS[g]
        cs = slice(g * gw, (g + 1) * gw)
        n_rows = halo + tm
        src, valid_from, lvl, d = None, 0, 0, 1
        while 2 * d < w:
            s = d * rows_per_tok
            a = -(-s // SUBLANES) * SUBLANES
            dst = wbuf.at[g * n_lvl + lvl]
            if src is None:
                dst[a:n_rows, :] = ubuf[a:n_rows, cs] + ubuf[a - s:n_rows - s, cs]
            else:
                dst[a:n_rows, :] = src[a:n_rows, :] + src[a - s:n_rows - s, :]
            valid_from = max(a, valid_from + s)
            src, lvl, d = dst, lvl + 1, 2 * d
        s = d * rows_per_tok
        assert halo - s >= valid_from
        if src is None:
            acc = ubuf[halo:n_rows, cs] + ubuf[halo - s:n_rows - s, cs]
        else:
            acc = src[halo:n_rows, :] + src[halo - s:n_rows - s, :]
        if pos0 >= w - 1:
            cnt = float(w)
        else:
            assert rows_per_tok == 1
            row = lax.broadcasted_iota(jnp.int32, (tm, 1), 0) + (t * tm + pos0 + 1)
            cnt = jnp.minimum(float(w), row.astype(F32))
        return (acc / cnt - u[:, cs]).astype(BF16)

    def key_heads(kt_piece, c):
        outs = []
        for hh in range(cw // hd):
            rs = slice(c * cw + hh * hd, c * cw + (hh + 1) * hd)
            kh = kt_piece[hh * hd:(hh + 1) * hd]
            msk = jnp.sum(kh * kh, axis=0, keepdims=True) * (1.0 / hd)
            outs.append(kh * lax.rsqrt(msk + EPS) * gk_ref[rs])
            kt_ref[0, rs, :] = outs[-1]
        return jnp.concatenate(outs, axis=0)

    n_kc = d_att // cw
    pooled, kn = [], []
    for c in range(n_kc):
        ktc = _dot_nt(wkt_ref[c * cw:(c + 1) * cw, :], h)
        pooled.append(window(c))
        side()
        kn.append((ktc, c))
    for c in range(n_kc):
        vt_ref[0, c * cw:(c + 1) * cw, :] = _dot_nt(wvt_ref[c * cw:(c + 1) * cw, :], h)
        kn[c] = key_heads(*kn[c])
        side()
    kn = jnp.concatenate(kn, axis=0)
    qs = []
    for c in range(n_kc):
        qs.append(_dot(h, wq_ref[:, c * cw:(c + 1) * cw]))
        if n_kc + c < len(POOL_WINDOWS):
            pooled.append(window(n_kc + c))
        side()
    for g in range(len(pooled), len(POOL_WINDOWS)):
        pooled.append(window(g))
    k2_ref[0] = jnp.transpose(kn).astype(BF16)

    @pl.when(t == 0)
    def _():
        kmt_ref[0] = jnp.zeros((d_att, LANES), F32)

    if block_means:
        lane_b = lax.broadcasted_iota(jnp.int32, (d_att, LANES), 1)
        for bb in range(tm // MOBA_BLOCK):
            part = kn[:, bb * MOBA_BLOCK:bb * MOBA_BLOCK + LANES]
            for c in range(1, MOBA_BLOCK // LANES):
                part = part + kn[:, bb * MOBA_BLOCK + c * LANES:bb * MOBA_BLOCK + (c + 1) * LANES]
            col = jnp.sum(part, axis=-1, keepdims=True) * (1.0 / MOBA_BLOCK)
            kmt_ref[0] = jnp.where(lane_b == t * (tm // MOBA_BLOCK) + bb, col, kmt_ref[0])

    ys = [_dot(pooled[g], wpg_ref[g]) for g in range(len(POOL_WINDOWS))]
    split = []
    for c in range(n_kc):
        qq = qs[c] * qs[c]
        hi = qq.astype(BF16)
        split.append((hi, (qq - hi.astype(F32)).astype(BF16)))
    side()
    y = (jnp.concatenate(ys, axis=1) * psc_ref[...]).astype(BF16)
    a = []
    for c in range(d_model // cw):
        a.append(_dot(y, wbp_ref[:, c * cw:(c + 1) * cw]))
        if c < n_kc:
            msq = _dot(split[c][0], bd_ref[...]) + _dot(split[c][1], bd_ref[...])
            cs = slice(c * cw, (c + 1) * cw)
            q_ref[0, :, cs] = (qs[c] * lax.rsqrt(msq + EPS) * gq_ref[:, cs] * scale).astype(BF16)
        side()

    n_gc = 2 * d_model // cw
    gate = [None] * n_gc

    def finish_gate(c):
        sg = _sigmoid(gate[c])
        lo_c = c * cw
        if lo_c < d_model:
            ag_ref[0, :, lo_c:lo_c + cw] = (sg * a[c]).astype(BF16)
        else:
            gb_ref[0, :, lo_c - d_model:lo_c - d_model + cw] = sg.astype(BF16)

    for c in range(n_gc):
        gate[c] = _dot(h, wg_ref[:, c * cw:(c + 1) * cw])
        if c > 0:
            finish_gate(c - 1)
        side()
    finish_gate(n_gc - 1)
    side(len(todo))

    @pl.when(t == nt - 1)
    def _():
        keep = POOL_CTX * rows_per_tok
        pool_ref[0] = ubuf[halo + tm - keep:halo + tm, :]

    if stream is not None:
        @pl.when(step + 2 < n_steps)
        def _():
            for cp in batch_copies(step + 2, side_set):
                cp.start()


def _inproj_call(x, mod, state, wts, *, tm, rows_per_tok, halo, pos0, n_heads, hd, block_means, pages=None):
    n, s, d = x.shape
    nt = s // tm
    d_att = n_heads * hd
    pw = wts["w_u"].shape[1]
    keep = POOL_CTX * rows_per_tok
    stream = None
    if pages is not None:
        page_table, ck_t = pages
        n_seq, n_pages = page_table.shape
        _, km_rows, page = ck_t.shape
        gpsq = n_pages // _KM_GROUP
        total = n_seq * gpsq
        gps = total // (n * nt)
        assert n_pages % _KM_GROUP == 0 and total == gps * n * nt and gps >= 1 and km_rows % _KM_ROWS == 0
        assert gpsq % gps == 0
        stream = (gps, gpsq, total, MOBA_BLOCK // page)
    body = functools.partial(_inproj_body, tm=tm, rows_per_tok=rows_per_tok, halo=halo, pos0=pos0, nt=nt,
                             n_heads=n_heads, hd=hd, scale=hd ** -0.5 * LOG2E, block_means=block_means, stream=stream)
    rm = mod.shape[2]

    def cs(shape):
        nd = len(shape)
        return pl.BlockSpec(shape, lambda *_: (0,) * nd, pipeline_mode=pl.Buffered(1))

    def bs(shape, fn):
        return pl.BlockSpec(shape, lambda i, t, *_: fn(i, t))

    in_specs = [
        bs((1, tm, d), lambda i, t: (i, t, 0)),
        bs((1, N_MOD, rm, d), lambda i, t: (i, 0, 0, 0)),
        cs((1, d)),
        cs(wts["w_u"].shape), cs(wts["w_q"].shape), cs(wts["w_kt"].shape),
        cs(wts["w_vt"].shape), cs(wts["w_g"].shape),
        cs((1, d_att)), cs((d_att, 1)), cs((MXU_DIM, MXU_DIM)),
        cs(wts["w_pg"].shape), cs((1, pw)), cs(wts["w_bp"].shape),
        bs((1, halo, pw), lambda i, t: (i, 0, 0)),
    ]
    out_specs = [
        bs((1, d_att, tm), lambda i, t: (i, 0, t)),
        bs((1, d_att, tm), lambda i, t: (i, 0, t)),
        bs((1, tm, d_att), lambda i, t: (i, t, 0)),
        bs((1, tm, d_att), lambda i, t: (i, t, 0)),
        bs((1, d_att, LANES), lambda i, t: (i, 0, 0)),
        bs((1, tm, d), lambda i, t: (i, t, 0)),
        bs((1, tm, d), lambda i, t: (i, t, 0)),
        bs((1, keep, pw), lambda i, t: (i, 0, 0)),
    ]
    out_shape = [
        jax.ShapeDtypeStruct((n, d_att, s), F32),
        jax.ShapeDtypeStruct((n, d_att, s), F32),
        jax.ShapeDtypeStruct((n, s, d_att), BF16),
        jax.ShapeDtypeStruct((n, s, d_att), BF16),
        jax.ShapeDtypeStruct((n, d_att, LANES), F32),
        jax.ShapeDtypeStruct((n, s, d), BF16),
        jax.ShapeDtypeStruct((n, s, d), BF16),
        jax.ShapeDtypeStruct((n, keep, pw), F32),
    ]
    n_lvl = max(len(POOL_WINDOWS) - 1, 1)
    scratch = [pltpu.VMEM((halo + tm, pw), F32), pltpu.VMEM((len(POOL_WINDOWS) * n_lvl, halo + tm, LANES), F32)]
    args = [x, mod, wts["g_mix"], wts["w_u"], wts["w_q"], wts["w_kt"], wts["w_vt"], wts["w_g"], wts["g_q"], wts["g_k"],
            wts["bd"], wts["w_pg"], wts["p_scale"], wts["w_bp"], state]
    n_prefetch = 0
    if stream is not None:
        gps, gpsq = stream[0], stream[1]
        in_specs.append(pl.BlockSpec(memory_space=pl.ANY))
        out_specs.append(bs((1, km_rows, LANES), lambda i, t: ((i * nt + t) * gps // gpsq, 0, 0)))
        out_shape.append(jax.ShapeDtypeStruct((n_seq, km_rows, LANES), F32))
        scratch += [pltpu.VMEM((2 * gps * _KM_GROUP, km_rows, page), F32), pltpu.SemaphoreType.DMA((2,))]
        args = [page_table] + args + [ck_t]
        n_prefetch = 1
    grid_spec = pltpu.PrefetchScalarGridSpec(num_scalar_prefetch=n_prefetch, grid=(n, nt), in_specs=in_specs,
                                             out_specs=out_specs, scratch_shapes=scratch)
    return pl.pallas_call(
        body, grid_spec=grid_spec, out_shape=out_shape,
        compiler_params=pltpu.CompilerParams(dimension_semantics=("arbitrary", "arbitrary"),
                                             vmem_limit_bytes=VMEM_LIMIT),
        name="mixer_inproj",
    )(*args)


def _attn_prompt_body(q_ref, k2_ref, kmt_ref, vt_ref, o_ref, va_ref, qh_ref, pen_ref, p_ref,
                      *, seq, nb, hd):
    blk = MOBA_BLOCK
    nbp = SUBLANES
    q2 = q_ref[0]
    ones_rows = (lax.broadcasted_iota(jnp.int32, (SUBLANES, seq), 0) == 0).astype(F32)
    for hh in range(2):
        va = jnp.concatenate([vt_ref[0, hh * hd:(hh + 1) * hd, :], ones_rows, jnp.zeros((SUBLANES, seq), F32)], axis=0)
        va_ref[hh] = va.astype(BF16)
    km = jnp.transpose(kmt_ref[0])
    lane_q = lax.broadcasted_iota(jnp.int32, (seq, 2 * hd), 1)
    lane_m = lax.broadcasted_iota(jnp.int32, (2 * SUBLANES, 2 * hd), 1)
    tpos = lax.broadcasted_iota(jnp.int32, (nbp, seq), 1)
    bidx = lax.broadcasted_iota(jnp.int32, (nbp, seq), 0)
    qblk = jnp.zeros((nbp, seq), jnp.int32)
    for b in range(1, nb):
        qblk = qblk + (tpos >= b * blk).astype(jnp.int32)
    for hh in range(2):
        in_head = (lane_m >= hh * hd) & (lane_m < (hh + 1) * hd)
        kmh = jnp.where(in_head, km[0:2 * SUBLANES], 0.0).astype(BF16)
        sct = _dot_nt(kmh, q2)[0:nbp]
        rank = jnp.zeros((nbp, seq), F32)
        for bp in range(nb):
            rowv = sct[bp:bp + 1, :]
            beats = (rowv > sct) | ((rowv == sct) & (bp < bidx))
            rank = rank + jnp.where(beats & (bp < qblk), 1.0, 0.0)
        keep = (bidx < qblk) & (rank < MOBA_TOPK)
        pen_ref[hh] = jnp.where(keep, 0.0, NEG)
        own_lanes = (lane_q >= hh * hd) & (lane_q < (hh + 1) * hd)
        qh_ref[hh] = jnp.where(own_lanes, q2, jnp.zeros_like(q2))

    kr = lax.broadcasted_iota(jnp.int32, (blk, blk), 0)
    qc = lax.broadcasted_iota(jnp.int32, (blk, blk), 1)
    causal = kr <= qc
    items = [(i, hh) for i in range(nb) for hh in range(2)]
    m_of, zero_of, o_of, s_of = {}, {}, {}, {}

    def base_of(i):
        return blk * (i * (i + 1) // 2)

    def logits_piece(k, j):
        i, hh = items[k]
        qs = slice(i * blk, (i + 1) * blk)
        sv = _dot_nt(k2_ref[0, j * blk:(j + 1) * blk, :], qh_ref[hh, qs, :])
        s_of[k, j] = sv
        if j == i:
            mj = jnp.max(jnp.where(causal, sv, NEG), axis=0, keepdims=True)
        else:
            mj = jnp.max(sv, axis=0, keepdims=True) + pen_ref[hh, j:j + 1, qs]
        m_of[k] = mj if j == 0 else jnp.maximum(m_of[k], mj)

    def exp_piece(k, j):
        i, hh = items[k]
        qs = slice(i * blk, (i + 1) * blk)
        if j == 0:
            zero_of[k] = [jnp.zeros((1, blk), F32)] * _EXP_DEPTH
        shift = m_of[k] if j == i else m_of[k] - pen_ref[hh, j:j + 1, qs]
        sv = s_of.pop((k, j))
        for c in range(blk // _EXP_ROWS):
            r0 = base_of(i) + j * blk + c * _EXP_ROWS
            x = sv[c * _EXP_ROWS:(c + 1) * _EXP_ROWS]
            if j == i:
                x = jnp.where(causal[c * _EXP_ROWS:(c + 1) * _EXP_ROWS], x, NEG)
            p = jnp.exp2(x - (shift + zero_of[k][c % _EXP_DEPTH]))
            p_ref[hh, r0:r0 + _EXP_ROWS, :] = p.astype(BF16)
            zero_of[k][c % _EXP_DEPTH] = jnp.minimum(p[_EXP_ROWS - 1:_EXP_ROWS], 0.0)

    def values(k):
        i, hh = items[k]
        ln = (i + 1) * blk
        ot = _dot(va_ref[hh, :, 0:ln], p_ref[hh, base_of(i):base_of(i) + ln, :])
        o_of[k] = ot[0:hd] / ot[hd:hd + 1]
        if hh == 1:
            both = jnp.concatenate([o_of.pop(k - 1), o_of.pop(k)], axis=0)
            o_ref[0, i * blk:(i + 1) * blk, :] = jnp.transpose(both).astype(BF16)

    for j in range(items[0][0] + 1):
        logits_piece(0, j)
    for k in range(len(items)):
        ahead = [(k + 1, j) for j in range(items[k + 1][0] + 1)] if k + 1 < len(items) else []
        for j in range(items[k][0] + 1):
            exp_piece(k, j)
            if ahead:
                logits_piece(*ahead.pop(0))
        for kj in ahead:
            logits_piece(*kj)
        values(k)


def _attn_prompt_call(q, k2, kmt, vt, *, n_heads, hd):
    n, s, d_att = q.shape
    nb = s // MOBA_BLOCK
    body = functools.partial(_attn_prompt_body, seq=s, nb=nb, hd=hd)
    tri = MOBA_BLOCK * (nb * (nb + 1) // 2)
    tok_spec = pl.BlockSpec((1, s, 2 * hd), lambda i, p: (i, 0, p))
    return pl.pallas_call(
        body, grid=(n, n_heads // 2),
        in_specs=[tok_spec, tok_spec,
                  pl.BlockSpec((1, 2 * hd, LANES), lambda i, p: (i, p, 0)),
                  pl.BlockSpec((1, 2 * hd, s), lambda i, p: (i, p, 0))],
        out_specs=tok_spec,
        out_shape=jax.ShapeDtypeStruct((n, s, d_att), BF16),
        scratch_shapes=[pltpu.VMEM((2, hd + 2 * SUBLANES, s), BF16),
                        pltpu.VMEM((2, s, 2 * hd), BF16), pltpu.VMEM((2, SUBLANES, s), F32),
                        pltpu.VMEM((2, tri, MOBA_BLOCK), BF16)],
        compiler_params=pltpu.CompilerParams(dimension_semantics=("arbitrary", "arbitrary"),
                                             vmem_limit_bytes=VMEM_LIMIT),
        name="moba_prompt",
    )(q, k2, kmt, vt)


def _post_body(x_ref, o_ref, ag_ref, gb_ref, mod_ref, wba_ref, wout_ref, gffn_ref, wup_ref, wconv_ref, bconv_ref,
               wdown_ref, cst_ref, y_ref, conv_ref, gbuf, act_ref, *, tm, rows_per_tok, halo, nt, pc, down_groups, d_ff):
    t = pl.program_id(1)
    x = x_ref[0]
    gt_m = mod_ref[0, 2]
    sh_f = mod_ref[0, 3]
    sc_f = mod_ref[0, 4]
    gt_f = mod_ref[0, 5]
    oa = _dot(o_ref[0], wba_ref[...])
    merged = ag_ref[0].astype(F32) + gb_ref[0].astype(F32) * oa
    x1 = x + gt_m * _dot(merged.astype(BF16), wout_ref[...])
    ms = jnp.mean(x1 * x1, axis=-1, keepdims=True)
    h2 = x1 * lax.rsqrt(ms + EPS) * gffn_ref[...]
    h2 = (h2 * (1.0 + sc_f) + sh_f).astype(BF16)

    @pl.when(t == 0)
    def _():
        gbuf[0:halo, :] = cst_ref[0]

    @pl.when(t > 0)
    def _():
        gbuf[0:halo, :] = gbuf[tm:tm + halo, :]

    r = rows_per_tok
    n_pc = d_ff // pc

    def up(c):
        cs = slice(c * pc, (c + 1) * pc)
        fg = _dot(h2, wup_ref[:, cs])
        fv = _dot(h2, wup_ref[:, d_ff + c * pc:d_ff + (c + 1) * pc])
        gbuf[halo:halo + tm, cs] = fg
        return fg, fv

    def activate(c, fg, fv):
        cs = slice(c * pc, (c + 1) * pc)
        conv = bconv_ref[:, cs]
        conv = conv + gbuf[halo - 2 * r:halo - 2 * r + tm, cs] * wconv_ref[0:1, cs]
        conv = conv + gbuf[halo - r:halo - r + tm, cs] * wconv_ref[1:2, cs]
        conv = conv + fg * wconv_ref[2:3, cs]
        act_ref[:, cs] = (conv * _sigmoid(conv) * fv).astype(BF16)

    bounds = [round(g * n_pc / down_groups) * pc for g in range(down_groups + 1)]
    acc = None
    nxt = up(0)
    for c in range(n_pc):
        cur = nxt
        if c + 1 < n_pc:
            nxt = up(c + 1)
        activate(c, *cur)
        if (c + 1) * pc in bounds[1:]:
            lo = bounds[bounds.index((c + 1) * pc) - 1]
            part = _dot(act_ref[:, lo:(c + 1) * pc], wdown_ref[lo:(c + 1) * pc, :])
            acc = part if acc is None else acc + part
    y_ref[0] = x1 + gt_f * acc

    @pl.when(t == nt - 1)
    def _():
        keep = (CONV_W - 1) * r
        conv_ref[0] = gbuf[halo + tm - keep:halo + tm, :]


def _post_call(x, o, ag, gb, mod, cstate, wts, *, tm, rows_per_tok, halo, pc, down_groups):
    n, s, d = x.shape
    nt = s // tm
    d_att = o.shape[-1]
    d_ff = wts["w_down"].shape[0]
    keep = (CONV_W - 1) * rows_per_tok
    rm = mod.shape[2]
    body = functools.partial(_post_body, tm=tm, rows_per_tok=rows_per_tok, halo=halo, nt=nt, pc=pc,
                             down_groups=down_groups, d_ff=d_ff)
    row_spec = lambda w: pl.BlockSpec((1, tm, w), lambda i, t: (i, t, 0))
    in_specs = [
        row_spec(d), row_spec(d_att), row_spec(d), row_spec(d),
        pl.BlockSpec((1, N_MOD, rm, d), lambda i, t: (i, 0, 0, 0)),
        _const_spec(wts["w_ba"].shape), _const_spec(wts["w_out"].shape), _const_spec((1, d)),
        _const_spec(wts["w_up"].shape), _const_spec((CONV_W, d_ff)), _const_spec((1, d_ff)),
        _const_spec(wts["w_down"].shape),
        pl.BlockSpec((1, halo, d_ff), lambda i, t: (i, 0, 0)),
    ]
    out_specs = [row_spec(d), pl.BlockSpec((1, keep, d_ff), lambda i, t: (i, 0, 0))]
    out_shape = [jax.ShapeDtypeStruct((n, s, d), F32), jax.ShapeDtypeStruct((n, keep, d_ff), F32)]
    return pl.pallas_call(
        body, grid=(n, nt), in_specs=in_specs, out_specs=out_specs, out_shape=out_shape,
        scratch_shapes=[pltpu.VMEM((halo + tm, d_ff), F32), pltpu.VMEM((tm, d_ff), BF16)],
        compiler_params=pltpu.CompilerParams(dimension_semantics=("arbitrary", "arbitrary"),
                                             vmem_limit_bytes=VMEM_LIMIT),
        name="merge_ffn",
    )(x, o, ag, gb, mod, wts["w_ba"], wts["w_out"], wts["g_ffn"], wts["w_up"], wts["w_conv"], wts["b_conv"],
      wts["w_down"], cstate)


def _kmean_body(pt_ref, ck_ref, km_ref, buf, sem, *, n_seq, n_pages, ppb):
    n = pl.program_id(0)
    rows = km_ref.shape[1]
    bpg = _KM_GROUP // ppb
    n_groups = n_pages // _KM_GROUP

    def copy(seq, p, slot):
        return pltpu.make_async_copy(ck_ref.at[pt_ref[seq, p]], buf.at[slot], sem.at[slot])

    def start_group(seq, g, half):
        for k in range(_KM_GROUP):
            copy(seq, g * _KM_GROUP + k, half * _KM_GROUP + k).start()

    @pl.when(n == 0)
    def _():
        start_group(0, 0, 0)
        start_group(0, 1, 1)

    km_ref[0] = jnp.zeros((rows, LANES), F32)
    lane = lax.broadcasted_iota(jnp.int32, (_KM_ROWS, LANES), 1)

    def pair(g2, carry):
        for half in range(2):
            g = g2 * 2 + half
            for k in range(_KM_GROUP):
                copy(n, g * _KM_GROUP + k, half * _KM_GROUP + k).wait()
            for rc in range(rows // _KM_ROWS):
                rs = slice(rc * _KM_ROWS, (rc + 1) * _KM_ROWS)
                upd = jnp.zeros((_KM_ROWS, LANES), F32)
                for bb in range(bpg):
                    slot = half * _KM_GROUP + bb * ppb
                    x = buf[slot, rs, :]
                    for pp in range(1, ppb):
                        x = x + buf[slot + pp, rs, :]
                    col = jnp.sum(x, axis=-1, keepdims=True)
                    upd = jnp.where(lane == g * bpg + bb, col, upd)
                km_ref[0, rs, :] = km_ref[0, rs, :] + upd * (1.0 / MOBA_BLOCK)
            nxt = g + 2

            @pl.when(nxt < n_groups)
            def _():
                start_group(n, nxt, half)

            @pl.when((nxt >= n_groups) & (n + 1 < n_seq))
            def _():
                start_group(n + 1, nxt - n_groups, half)
        return carry

    lax.fori_loop(0, n_groups // 2, pair, 0)


def _kmean_call(page_table, ck_t):
    n, n_pages = page_table.shape
    _, rows, page = ck_t.shape
    ppb = MOBA_BLOCK // page
    assert n_pages % (2 * _KM_GROUP) == 0 and rows % _KM_ROWS == 0
    body = functools.partial(_kmean_body, n_seq=n, n_pages=n_pages, ppb=ppb)
    grid_spec = pltpu.PrefetchScalarGridSpec(
        num_scalar_prefetch=1, grid=(n,),
        in_specs=[pl.BlockSpec(memory_space=pl.ANY)],
        out_specs=pl.BlockSpec((1, rows, LANES), lambda i, pt: (i, 0, 0)),
        scratch_shapes=[pltpu.VMEM((2 * _KM_GROUP, rows, page), F32), pltpu.SemaphoreType.DMA((2 * _KM_GROUP,))],
    )
    return pl.pallas_call(
        body, grid_spec=grid_spec,
        out_shape=jax.ShapeDtypeStruct((n, rows, LANES), F32),
        compiler_params=pltpu.CompilerParams(dimension_semantics=("arbitrary",), vmem_limit_bytes=VMEM_LIMIT),
        name="cache_block_means",
    )(page_table, ck_t)


def _topk_body(q_ref, km_ref, sel_ref, *, n_heads, hd, n_blk, top, unroll):
    n_seq = q_ref.shape[0]
    rows = n_heads * SUBLANES
    lane_q = lax.broadcasted_iota(jnp.int32, (SUBLANES, n_heads * hd), 1)
    lane = lax.broadcasted_iota(jnp.int32, (unroll * rows, LANES), 1).astype(F32)

    def per_group(ig, carry):
        scs = []
        for k in range(unroll):
            i = ig * unroll + k
            q8 = q_ref[i]
            qbd = jnp.concatenate(
                [jnp.where((lane_q >= hh * hd) & (lane_q < (hh + 1) * hd), q8, 0.0) for hh in range(n_heads)], axis=0)
            scs.append(_dot(qbd.astype(BF16), km_ref[i].astype(BF16)))
        sc = jnp.where(lane < n_blk, jnp.concatenate(scs, axis=0), -jnp.inf)
        out = jnp.zeros((unroll * rows, LANES), F32)
        for r in range(top):
            m = jnp.max(sc, axis=-1, keepdims=True)
            idx = jnp.min(jnp.where(sc == m, lane, float(LANES)), axis=-1, keepdims=True)
            out = jnp.where(lane == r, idx, out)
            sc = jnp.where(lane == idx, -jnp.inf, sc)
        out = out.astype(jnp.int32)
        for k in range(unroll):
            sel_ref[ig * unroll + k] = out[k * rows:(k + 1) * rows]
        return carry

    lax.fori_loop(0, n_seq // unroll, per_group, 0)


def _topk_call(q8, kmt, *, n_heads, hd, n_blk, top):
    n = q8.shape[0]
    rows = n_heads * SUBLANES
    unroll = max(u for u in (8, 4, 2, 1) if n % u == 0)
    body = functools.partial(_topk_body, n_heads=n_heads, hd=hd, n_blk=n_blk, top=top, unroll=unroll)
    return pl.pallas_call(
        body, grid=(1,),
        in_specs=[pl.BlockSpec(q8.shape, lambda i: (0, 0, 0)), pl.BlockSpec(kmt.shape, lambda i: (0, 0, 0))],
        out_specs=pl.BlockSpec((n, rows, LANES), lambda i: (0, 0, 0)),
        out_shape=jax.ShapeDtypeStruct((n, rows, LANES), jnp.int32),
        compiler_params=pltpu.CompilerParams(vmem_limit_bytes=VMEM_LIMIT),
        name="block_topk",
    )(q8, kmt)


def _attn_sample_body(pt_ref, sel_ref, qt_ref, knt_ref, vnt_ref, ck_ref, cv_ref, o_ref, kbuf, vbuf, lg_ref, p_ref, sem,
                      *, n_seq, n_heads, hd, n_tok, top, ppb):
    n = pl.program_id(0)
    n_sel = top * ppb
    n_grp = n_heads * n_tok
    slot = lax.rem(n, 2)

    def gather(seq, sl):
        for hh in range(n_heads):
            for s in range(n_tok):
                g = hh * n_tok + s
                for j in range(n_sel):
                    r, pp = divmod(j, ppb)
                    page = pt_ref[seq, sel_ref[seq, g * top + r] * ppb + pp]
                    pltpu.make_async_copy(ck_ref.at[page, hh], kbuf.at[sl, g, j], sem.at[sl]).start()
                    pltpu.make_async_copy(cv_ref.at[page, hh], vbuf.at[sl, g, j], sem.at[sl]).start()

    @pl.when(n == 0)
    def _():
        gather(0, 0)

    @pl.when(n + 1 < n_seq)
    def _():
        gather(n + 1, 1 - slot)

    for g in range(n_grp):
        for j in range(n_sel):
            pltpu.make_async_copy(ck_ref.at[0, 0], kbuf.at[slot, g, j], sem.at[slot]).wait()
            pltpu.make_async_copy(cv_ref.at[0, 0], vbuf.at[slot, g, j], sem.at[slot]).wait()

    lane = lax.broadcasted_iota(jnp.int32, (1, LANES), 1)
    pad_rows = [jnp.full((1, LANES), NEG, F32)] * (SUBLANES - n_sel - 1)
    for hh in range(n_heads):
        rs = slice(hh * hd, (hh + 1) * hd)
        knt = knt_ref[0, rs, :]
        for s in range(n_tok):
            g = hh * n_tok + s
            qb = jnp.broadcast_to(qt_ref[0, rs, s:s + 1], (hd, LANES))
            rows = [jnp.sum(kbuf[slot, g, j] * qb, axis=0, keepdims=True) for j in range(n_sel)]
            own = jnp.sum(knt * qb, axis=0, keepdims=True)
            rows.append(jnp.where(lane <= s, own, NEG))
            lg_ref[g] = jnp.concatenate(rows + pad_rows, axis=0)

    lg = lg_ref[...]
    m = jnp.max(jnp.max(lg, axis=1), axis=-1, keepdims=True)
    mb = jnp.broadcast_to(m, (n_grp, LANES))
    for g in range(n_grp):
        p_ref[g] = jnp.exp2(lg_ref[g] - mb[g:g + 1, :])
    l = jnp.sum(jnp.sum(p_ref[...], axis=1), axis=-1, keepdims=True)
    inv = jnp.broadcast_to(1.0 / l, (n_grp, LANES))

    lane_o = lax.broadcasted_iota(jnp.int32, (hd, LANES), 1)
    for hh in range(n_heads):
        rs = slice(hh * hd, (hh + 1) * hd)
        vnt = vnt_ref[0, rs, :]
        out_h = jnp.zeros((hd, LANES), F32)
        for s in range(n_tok):
            g = hh * n_tok + s
            pg = p_ref[g] * inv[g:g + 1, :]
            acc = vnt * pg[n_sel:n_sel + 1, :]
            for j in range(n_sel):
                acc = acc + vbuf[slot, g, j] * pg[j:j + 1, :]
            col = jnp.sum(acc, axis=-1, keepdims=True)
            out_h = jnp.where(lane_o == s, col, out_h)
        o_ref[0, rs, :] = out_h


def _attn_sample_call(page_table, sel, qt, knt, vnt, ck_t, cv_t, *, n_heads, hd, n_tok, top):
    n = page_table.shape[0]
    page = ck_t.shape[-1]
    ppb = MOBA_BLOCK // page
    d_att = n_heads * hd
    n_grp = n_heads * n_tok
    assert top * ppb + 1 <= SUBLANES
    body = functools.partial(_attn_sample_body, n_seq=n, n_heads=n_heads, hd=hd, n_tok=n_tok, top=top, ppb=ppb)
    blk = pl.BlockSpec((1, d_att, LANES), lambda i, pt, sl: (i, 0, 0))
    grid_spec = pltpu.PrefetchScalarGridSpec(
        num_scalar_prefetch=2, grid=(n,),
        in_specs=[blk, blk, blk, pl.BlockSpec(memory_space=pl.ANY), pl.BlockSpec(memory_space=pl.ANY)],
        out_specs=blk,
        scratch_shapes=[pltpu.VMEM((2, n_grp, top * ppb, hd, page), F32),
                        pltpu.VMEM((2, n_grp, top * ppb, hd, page), F32),
                        pltpu.VMEM((n_grp, SUBLANES, LANES), F32),
                        pltpu.VMEM((n_grp, SUBLANES, LANES), F32),
                        pltpu.SemaphoreType.DMA((2,))],
    )
    return pl.pallas_call(
        body, grid_spec=grid_spec,
        out_shape=jax.ShapeDtypeStruct((n, d_att, LANES), F32),
        compiler_params=pltpu.CompilerParams(dimension_semantics=("arbitrary",), vmem_limit_bytes=VMEM_LIMIT),
        name="moba_sample",
    )(page_table, sel, qt, knt, vnt, ck_t, cv_t)


def _layer_weights(l, g_norm_mix, w_in, g_q, g_k, w_pool_group, pool_scale, w_branch_pool, w_branch_attn, w_out,
                   g_norm_ffn, w_up, w_conv, b_conv, w_down, n_heads, hd):
    d = w_in.shape[1]
    pw = w_pool_group.shape[1] * w_pool_group.shape[2]
    d_att = n_heads * hd
    wi = w_in[l].astype(BF16)
    c0, c1, c2, c3 = pw, pw + d_att, pw + 2 * d_att, pw + 3 * d_att
    blk = jnp.arange(MXU_DIM) // hd
    return {
        "g_mix": g_norm_mix[l].reshape(1, d),
        "w_u": wi[:, :c0], "w_q": wi[:, c0:c1], "w_kt": wi[:, c1:c2].T, "w_vt": wi[:, c2:c3].T, "w_g": wi[:, c3:],
        "g_q": g_q[l].reshape(1, d_att), "g_k": g_k[l].reshape(d_att, 1),
        "bd": jnp.where(blk[:, None] == blk[None, :], 1.0 / hd, 0.0).astype(BF16),
        "w_pg": w_pool_group[l].astype(BF16), "p_scale": pool_scale[l].reshape(1, pw),
        "w_bp": w_branch_pool[l].astype(BF16),
        "w_ba": w_branch_attn[l].astype(BF16), "w_out": w_out[l].astype(BF16),
        "g_ffn": g_norm_ffn[l].reshape(1, d), "w_up": w_up[l].astype(BF16),
        "w_conv": w_conv[l], "b_conv": b_conv[l].reshape(1, -1), "w_down": w_down[l].astype(BF16),
    }


def kernel(x_prompt, x_sample, cache_k, cache_v, state_pool, state_ffn_conv, page_table, c_prompt, c_sample, w_ada, b_ada, g_norm_mix, w_in, g_q, g_k, w_pool_group, pool_scale, w_branch_pool, w_branch_attn, w_out, g_norm_ffn, w_up, w_conv, b_conv, w_down):
    nb_p, seq, d = x_prompt.shape
    nb_s, n_tok, _ = x_sample.shape
    depth, n_phys, page, n_heads, hd = cache_k.shape
    d_att = n_heads * hd
    pw = state_pool.shape[-1]
    d_ff = w_down.shape[1]
    n_pages = page_table.shape[1]
    past = n_pages * page
    assert MOBA_BLOCK % page == 0 and past % MOBA_BLOCK == 0 and seq % MOBA_BLOCK == 0
    assert n_tok <= SUBLANES
    n_blk = past // MOBA_BLOCK
    top = min(MOBA_TOPK, n_blk)
    assert 0 < n_blk <= LANES and seq // MOBA_BLOCK <= SUBLANES
    tm = 256
    tm_ffn = 512 if seq % 512 == 0 else tm
    pc, dg = MXU_DIM, 3
    assert d_ff % pc == 0
    rows_s = nb_s * n_tok

    y_p, y_s = x_prompt, x_sample.transpose(1, 0, 2).reshape(1, rows_s, d)
    halo_p = 4 * SUBLANES
    zero_pool = jnp.zeros((nb_p, halo_p, pw), F32)
    zero_conv = jnp.zeros((nb_p, SUBLANES, d_ff), F32)
    outs = [[] for _ in range(8)]
    for l in range(depth):
        wts = _layer_weights(l, g_norm_mix, w_in, g_q, g_k, w_pool_group, pool_scale, w_branch_pool, w_branch_attn,
                             w_out, g_norm_ffn, w_up, w_conv, b_conv, w_down, n_heads, hd)
        mod = _ada_call(jnp.concatenate([c_prompt, c_sample], axis=0), w_ada[l].astype(BF16), b_ada[l])
        mod_p = mod[:nb_p].reshape(nb_p, N_MOD, 1, d)
        mod_s = jnp.tile(mod[nb_p:].reshape(nb_s, N_MOD, d), (n_tok, 1, 1)).transpose(1, 0, 2)[None]

        ck_t = cache_k[l].transpose(0, 2, 3, 1)
        cv_t = cache_v[l].transpose(0, 2, 3, 1)
        ck_flat = ck_t.reshape(n_phys, d_att, page)
        gpsq, steps = n_pages // _KM_GROUP, nb_p * (seq // tm)
        gps = nb_s * gpsq // steps
        fuse = n_pages % _KM_GROUP == 0 and gps >= 1 and gps * steps == nb_s * gpsq and gpsq % gps == 0
        res = _inproj_call(y_p, mod_p, zero_pool, wts, tm=tm, rows_per_tok=1, halo=halo_p, pos0=0,
                           n_heads=n_heads, hd=hd, block_means=True, pages=(page_table, ck_flat) if fuse else None)
        kt, vt, q, k2, kmt, ag, gb, pool_p = res[:8]
        kmt_s = res[8] if fuse else _kmean_call(page_table, ck_flat)
        o = _attn_prompt_call(q, k2, kmt, vt, n_heads=n_heads, hd=hd)
        y_p, conv_p = _post_call(y_p, o, ag, gb, mod_p, zero_conv, wts, tm=tm_ffn, rows_per_tok=1, halo=SUBLANES, pc=pc,
                                 down_groups=dg)
        to_rows = lambda a: a.reshape(nb_p, n_heads, hd, seq).transpose(0, 3, 1, 2)
        outs[0].append(to_rows(kt)); outs[1].append(to_rows(vt)); outs[2].append(pool_p); outs[3].append(conv_p)

        pool_state = state_pool[l].transpose(1, 0, 2).reshape(1, POOL_CTX * nb_s, pw)
        conv_state = state_ffn_conv[l].transpose(1, 0, 2).reshape(1, (CONV_W - 1) * nb_s, d_ff)
        kt_s, vt_s, q_s, _, _, ag_s, gb_s, pool_s = _inproj_call(
            y_s, mod_s, pool_state, wts, tm=rows_s, rows_per_tok=nb_s, halo=POOL_CTX * nb_s, pos0=past,
            n_heads=n_heads, hd=hd, block_means=False)
        q_ns = q_s.reshape(n_tok, nb_s, d_att).transpose(1, 0, 2).astype(F32)
        q8 = jnp.pad(q_ns, ((0, 0), (0, SUBLANES - n_tok), (0, 0)))
        sel = _topk_call(q8, kmt_s, n_heads=n_heads, hd=hd, n_blk=n_blk, top=top)
        sel = sel.reshape(nb_s, n_heads, SUBLANES, LANES)[:, :, :n_tok, :top].reshape(nb_s, n_heads * n_tok * top)
        lane_pad = lambda a: jnp.pad(a.reshape(d_att, n_tok, nb_s).transpose(2, 0, 1),
                                     ((0, 0), (0, 0), (0, LANES - n_tok)))
        qt = jnp.pad(q_ns.transpose(0, 2, 1), ((0, 0), (0, 0), (0, LANES - n_tok)))
        ot = _attn_sample_call(page_table, sel, qt, lane_pad(kt_s[0]), lane_pad(vt_s[0]), ck_t, cv_t,
                               n_heads=n_heads, hd=hd, n_tok=n_tok, top=top)
        o_s = ot[:, :, :n_tok].transpose(2, 0, 1).reshape(1, rows_s, d_att).astype(BF16)
        y_s, conv_s = _post_call(y_s, o_s, ag_s, gb_s, mod_s, conv_state, wts, tm=rows_s, rows_per_tok=nb_s,
                                 halo=(CONV_W - 1) * nb_s, pc=pc, down_groups=dg)
        to_rows_s = lambda a: a[0].reshape(n_heads, hd, n_tok, nb_s).transpose(3, 2, 0, 1)
        outs[4].append(to_rows_s(kt_s)); outs[5].append(to_rows_s(vt_s))
        outs[6].append(pool_s.reshape(POOL_CTX, nb_s, pw).transpose(1, 0, 2))
        outs[7].append(conv_s.reshape(CONV_W - 1, nb_s, d_ff).transpose(1, 0, 2))

    y_s_out = y_s.reshape(n_tok, nb_s, d).transpose(1, 0, 2)
    st = [jnp.stack(v) for v in outs]
    return (y_p, y_s_out, st[0], st[1], st[2], st[3], st[4], st[5], st[6], st[7])
```

```python
import functools

import jax
import jax.numpy as jnp
from jax import lax
from jax.experimental import pallas as pl
from jax.experimental.pallas import tpu as pltpu

F32 = jnp.float32
BF16 = jnp.bfloat16

MOBA_BLOCK = 256
MOBA_TOPK = 3
POOL_WINDOWS = (2, 4, 8, 16)
POOL_CTX = max(POOL_WINDOWS) - 1
CONV_W = 3
N_MOD = 6
EPS = 1e-6
NEG = -1e30
LOG2E = 1.4426950408889634

LANES = 128
SUBLANES = 8
MXU_DIM = 256
VMEM_LIMIT = 60 * 1024 * 1024
_EXP_ROWS = 64
_EXP_DEPTH = 1
_KM_GROUP = 16
_KM_ROWS = 64


def _const_spec(shape):
    nd = len(shape)
    return pl.BlockSpec(shape, lambda *_: (0,) * nd, pipeline_mode=pl.Buffered(1))


def _dot(a, b):
    return jnp.dot(a, b, preferred_element_type=F32)


def _dot_nt(a, b):
    return lax.dot_general(a, b, (((1,), (1,)), ((), ())), preferred_element_type=F32)


def _sigmoid(x):
    return 1.0 / (1.0 + jnp.exp(-x))


def _ada_body(c_ref, w_ref, b_ref, o_ref):
    c = c_ref[...]
    a = (c * _sigmoid(c)).astype(BF16)
    o_ref[...] = _dot(a, w_ref[...]) + b_ref[...]


def _ada_call(c, w_ada, b_ada):
    n, d = c.shape
    dm = w_ada.shape[1]
    bn = d
    return pl.pallas_call(
        _ada_body,
        grid=(dm // bn,),
        in_specs=[pl.BlockSpec((n, d), lambda j: (0, 0)),
                  pl.BlockSpec((d, bn), lambda j: (0, j)),
                  pl.BlockSpec((1, bn), lambda j: (0, j))],
        out_specs=pl.BlockSpec((n, bn), lambda j: (0, j)),
        out_shape=jax.ShapeDtypeStruct((n, dm), F32),
        name="ada_mod",
    )(c, w_ada, b_ada.reshape(1, dm))


def _inproj_body(*refs, tm, rows_per_tok, halo, pos0, nt, n_heads, hd, scale, block_means, stream):
    if stream is not None:
        pt_ref, refs = refs[0], refs[1:]
    (x_ref, mod_ref, gmix_ref, win_ref, wkt_ref, wvt_ref, gq_ref, gk_ref, bd_ref, wpg_ref, psc_ref,
     wbp_ref, st_ref) = refs[:13]
    refs = refs[13:]
    if stream is not None:
        ck_ref, refs = refs[0], refs[1:]
    kt_ref, vt_ref, q_ref, k2_ref, kmt_ref, ag_ref, gb_ref, pool_ref = refs[:8]
    refs = refs[8:]
    if stream is not None:
        kms_ref, ubuf, wbuf, pbuf, psem = refs
    else:
        ubuf, wbuf = refs
    n_lvl = max(len(POOL_WINDOWS) - 1, 1)
    t = pl.program_id(1)
    d_att = n_heads * hd
    d_model = x_ref.shape[-1]
    cw = MXU_DIM

    todo = []
    if stream is not None:
        gps, gpsq, total, ppb = stream
        bpg = _KM_GROUP // ppb
        step = pl.program_id(0) * nt + t
        n_steps = total // gps
        km_rows = kms_ref.shape[1]
        lane_s = lax.broadcasted_iota(jnp.int32, (_KM_ROWS, LANES), 1)
        side_set = lax.rem(step, 2)

        def batch_copies(b, sset):
            out = []
            for k in range(gps):
                gidx = b * gps + k
                seq, g = lax.div(gidx, gpsq), lax.rem(gidx, gpsq)
                for j in range(_KM_GROUP):
                    slot = (sset * gps + k) * _KM_GROUP + j
                    out.append(pltpu.make_async_copy(ck_ref.at[pt_ref[seq, g * _KM_GROUP + j]], pbuf.at[slot],
                                                     psem.at[sset]))
            return out

        @pl.when(step == 0)
        def _():
            for b0 in range(min(2, n_steps)):
                for cp in batch_copies(jnp.int32(b0), b0):
                    cp.start()

        for slot in range(gps * _KM_GROUP):
            pltpu.make_async_copy(ck_ref.at[0], pbuf.at[side_set * gps * _KM_GROUP + slot], psem.at[side_set]).wait()

        @pl.when(lax.rem(step * gps, gpsq) == 0)
        def _():
            kms_ref[0] = jnp.zeros((km_rows, LANES), F32)

        def stream_reduce(k, rc):
            g = lax.rem(step * gps + k, gpsq)
            rs = slice(rc * _KM_ROWS, (rc + 1) * _KM_ROWS)
            upd = jnp.zeros((_KM_ROWS, LANES), F32)
            for bb in range(bpg):
                slot = (side_set * gps + k) * _KM_GROUP + bb * ppb
                xs = pbuf[slot, rs, :]
                for pp in range(1, ppb):
                    xs = xs + pbuf[slot + pp, rs, :]
                upd = jnp.where(lane_s == g * bpg + bb, jnp.sum(xs, axis=-1, keepdims=True), upd)
            kms_ref[0, rs, :] = kms_ref[0, rs, :] + upd * (1.0 / MOBA_BLOCK)

        todo = [(k, rc) for k in range(gps) for rc in range(km_rows // _KM_ROWS)]

    def side(n_pieces=1):
        for _ in range(n_pieces):
            if todo:
                stream_reduce(*todo.pop(0))

    x = x_ref[0]
    sh_m = mod_ref[0, 0]
    sc_m = mod_ref[0, 1]
    ms = jnp.mean(x * x, axis=-1, keepdims=True)
    h = x * lax.rsqrt(ms + EPS) * gmix_ref[...]
    h = (h * (1.0 + sc_m) + sh_m).astype(BF16)

    pw = wpg_ref.shape[0] * wpg_ref.shape[2]
    q0, g0 = pw, pw + 3 * d_att
    u = jnp.concatenate([_dot(h, win_ref[:, c * cw:(c + 1) * cw]) for c in range(pw // cw)], axis=1)

    @pl.when(t == 0)
    def _():
        ubuf[0:halo, :] = st_ref[0]

    @pl.when(t > 0)
    def _():
        ubuf[0:halo, :] = ubuf[tm:tm + halo, :]

    ubuf[halo:halo + tm, :] = u
    gw = LANES

    def window(g):
        w = POOL_WINDOWS[g]
        cs = slice(g * gw, (g + 1) * gw)
        n_rows = halo + tm
        src, valid_from, lvl, d = None, 0, 0, 1
        while 2 * d < w:
            s = d * rows_per_tok
            a = -(-s // SUBLANES) * SUBLANES
            dst = wbuf.at[g * n_lvl + lvl]
            if src is None:
                dst[a:n_rows, :] = ubuf[a:n_rows, cs] + ubuf[a - s:n_rows - s, cs]
            else:
                dst[a:n_rows, :] = src[a:n_rows, :] + src[a - s:n_rows - s, :]
            valid_from = max(a, valid_from + s)
            src, lvl, d = dst, lvl + 1, 2 * d
        s = d * rows_per_tok
        assert halo - s >= valid_from
        if src is None:
            acc = ubuf[halo:n_rows, cs] + ubuf[halo - s:n_rows - s, cs]
        else:
            acc = src[halo:n_rows, :] + src[halo - s:n_rows - s, :]
        if pos0 >= w - 1:
            cnt = float(w)
        else:
            assert rows_per_tok == 1
            row = lax.broadcasted_iota(jnp.int32, (tm, 1), 0) + (t * tm + pos0 + 1)
            cnt = jnp.minimum(float(w), row.astype(F32))
        return (acc / cnt - u[:, cs]).astype(BF16)

    def key_heads(kt_piece, c):
        outs = []
        for hh in range(cw // hd):
            rs = slice(c * cw + hh * hd, c * cw + (hh + 1) * hd)
            kh = kt_piece[hh * hd:(hh + 1) * hd]
            msk = jnp.sum(kh * kh, axis=0, keepdims=True) * (1.0 / hd)
            outs.append(kh * lax.rsqrt(msk + EPS) * gk_ref[rs])
            kt_ref[0, rs, :] = outs[-1]
        return jnp.concatenate(outs, axis=0)

    n_kc = d_att // cw
    pooled, kn = [], []
    for c in range(n_kc):
        ktc = _dot_nt(wkt_ref[c * cw:(c + 1) * cw, :], h)
        pooled.append(window(c))
        side()
        kn.append((ktc, c))
    for c in range(n_kc):
        vt_ref[0, c * cw:(c + 1) * cw, :] = _dot_nt(wvt_ref[c * cw:(c + 1) * cw, :], h)
        kn[c] = key_heads(*kn[c])
        side()
    kn = jnp.concatenate(kn, axis=0)
    qs = []
    for c in range(n_kc):
        qs.append(_dot(h, win_ref[:, q0 + c * cw:q0 + (c + 1) * cw]))
        if n_kc + c < len(POOL_WINDOWS):
            pooled.append(window(n_kc + c))
        side()
    for g in range(len(pooled), len(POOL_WINDOWS)):
        pooled.append(window(g))
    k2_ref[0] = jnp.transpose(kn).astype(BF16)

    @pl.when(t == 0)
    def _():
        kmt_ref[0] = jnp.zeros((d_att, LANES), F32)

    if block_means:
        lane_b = lax.broadcasted_iota(jnp.int32, (d_att, LANES), 1)
        for bb in range(tm // MOBA_BLOCK):
            part = kn[:, bb * MOBA_BLOCK:bb * MOBA_BLOCK + LANES]
            for c in range(1, MOBA_BLOCK // LANES):
                part = part + kn[:, bb * MOBA_BLOCK + c * LANES:bb * MOBA_BLOCK + (c + 1) * LANES]
            col = jnp.sum(part, axis=-1, keepdims=True) * (1.0 / MOBA_BLOCK)
            kmt_ref[0] = jnp.where(lane_b == t * (tm // MOBA_BLOCK) + bb, col, kmt_ref[0])

    ys = [_dot(pooled[g], wpg_ref[g]) for g in range(len(POOL_WINDOWS))]
    split = []
    for c in range(n_kc):
        qq = qs[c] * qs[c]
        hi = qq.astype(BF16)
        split.append((hi, (qq - hi.astype(F32)).astype(BF16)))
    side()
    y = (jnp.concatenate(ys, axis=1) * psc_ref[...]).astype(BF16)
    a = []
    for c in range(d_model // cw):
        a.append(_dot(y, wbp_ref[:, c * cw:(c + 1) * cw]))
        if c < n_kc:
            msq = _dot(split[c][0], bd_ref[...]) + _dot(split[c][1], bd_ref[...])
            cs = slice(c * cw, (c + 1) * cw)
            q_ref[0, :, cs] = (qs[c] * lax.rsqrt(msq + EPS) * gq_ref[:, cs] * scale).astype(BF16)
        side()

    n_gc = 2 * d_model // cw
    gate = [None] * n_gc

    def finish_gate(c):
        sg = _sigmoid(gate[c])
        lo_c = c * cw
        if lo_c < d_model:
            ag_ref[0, :, lo_c:lo_c + cw] = (sg * a[c]).astype(BF16)
        else:
            gb_ref[0, :, lo_c - d_model:lo_c - d_model + cw] = sg.astype(BF16)

    for c in range(n_gc):
        gate[c] = _dot(h, win_ref[:, g0 + c * cw:g0 + (c + 1) * cw])
        if c > 0:
            finish_gate(c - 1)
        side()
    finish_gate(n_gc - 1)
    side(len(todo))

    @pl.when(t == nt - 1)
    def _():
        keep = POOL_CTX * rows_per_tok
        pool_ref[0] = ubuf[halo + tm - keep:halo + tm, :]

    if stream is not None:
        @pl.when(step + 2 < n_steps)
        def _():
            for cp in batch_copies(step + 2, side_set):
                cp.start()


def _inproj_call(x, mod, state, wts, *, tm, rows_per_tok, halo, pos0, n_heads, hd, block_means, pages=None):
    n, s, d = x.shape
    nt = s // tm
    d_att = n_heads * hd
    pw = wts["p_scale"].shape[1]
    keep = POOL_CTX * rows_per_tok
    stream = None
    if pages is not None:
        page_table, ck_t = pages
        n_seq, n_pages = page_table.shape
        _, km_rows, page = ck_t.shape
        gpsq = n_pages // _KM_GROUP
        total = n_seq * gpsq
        gps = total // (n * nt)
        assert n_pages % _KM_GROUP == 0 and total == gps * n * nt and gps >= 1 and km_rows % _KM_ROWS == 0
        assert gpsq % gps == 0
        stream = (gps, gpsq, total, MOBA_BLOCK // page)
    body = functools.partial(_inproj_body, tm=tm, rows_per_tok=rows_per_tok, halo=halo, pos0=pos0, nt=nt,
                             n_heads=n_heads, hd=hd, scale=hd ** -0.5 * LOG2E, block_means=block_means, stream=stream)
    rm = mod.shape[2]

    def cs(shape):
        nd = len(shape)
        return pl.BlockSpec(shape, lambda *_: (0,) * nd, pipeline_mode=pl.Buffered(1))

    def bs(shape, fn):
        return pl.BlockSpec(shape, lambda i, t, *_: fn(i, t))

    in_specs = [
        bs((1, tm, d), lambda i, t: (i, t, 0)),
        bs((1, N_MOD, rm, d), lambda i, t: (i, 0, 0, 0)),
        cs((1, d)),
        cs(wts["w_in"].shape), cs(wts["w_kt"].shape), cs(wts["w_vt"].shape),
        cs((1, d_att)), cs((d_att, 1)), cs((MXU_DIM, MXU_DIM)),
        cs(wts["w_pg"].shape), cs((1, pw)), cs(wts["w_bp"].shape),
        bs((1, halo, pw), lambda i, t: (i, 0, 0)),
    ]
    out_specs = [
        bs((1, d_att, tm), lambda i, t: (i, 0, t)),
        bs((1, d_att, tm), lambda i, t: (i, 0, t)),
        bs((1, tm, d_att), lambda i, t: (i, t, 0)),
        bs((1, tm, d_att), lambda i, t: (i, t, 0)),
        bs((1, d_att, LANES), lambda i, t: (i, 0, 0)),
        bs((1, tm, d), lambda i, t: (i, t, 0)),
        bs((1, tm, d), lambda i, t: (i, t, 0)),
        bs((1, keep, pw), lambda i, t: (i, 0, 0)),
    ]
    out_shape = [
        jax.ShapeDtypeStruct((n, d_att, s), F32),
        jax.ShapeDtypeStruct((n, d_att, s), F32),
        jax.ShapeDtypeStruct((n, s, d_att), BF16),
        jax.ShapeDtypeStruct((n, s, d_att), BF16),
        jax.ShapeDtypeStruct((n, d_att, LANES), F32),
        jax.ShapeDtypeStruct((n, s, d), BF16),
        jax.ShapeDtypeStruct((n, s, d), BF16),
        jax.ShapeDtypeStruct((n, keep, pw), F32),
    ]
    n_lvl = max(len(POOL_WINDOWS) - 1, 1)
    scratch = [pltpu.VMEM((halo + tm, pw), F32), pltpu.VMEM((len(POOL_WINDOWS) * n_lvl, halo + tm, LANES), F32)]
    args = [x, mod, wts["g_mix"], wts["w_in"], wts["w_kt"], wts["w_vt"], wts["g_q"], wts["g_k"],
            wts["bd"], wts["w_pg"], wts["p_scale"], wts["w_bp"], state]
    n_prefetch = 0
    if stream is not None:
        gps, gpsq = stream[0], stream[1]
        in_specs.append(pl.BlockSpec(memory_space=pl.ANY))
        out_specs.append(bs((1, km_rows, LANES), lambda i, t: ((i * nt + t) * gps // gpsq, 0, 0)))
        out_shape.append(jax.ShapeDtypeStruct((n_seq, km_rows, LANES), F32))
        scratch += [pltpu.VMEM((2 * gps * _KM_GROUP, km_rows, page), F32), pltpu.SemaphoreType.DMA((2,))]
        args = [page_table] + args + [ck_t]
        n_prefetch = 1
    grid_spec = pltpu.PrefetchScalarGridSpec(num_scalar_prefetch=n_prefetch, grid=(n, nt), in_specs=in_specs,
                                             out_specs=out_specs, scratch_shapes=scratch)
    return pl.pallas_call(
        body, grid_spec=grid_spec, out_shape=out_shape,
        compiler_params=pltpu.CompilerParams(dimension_semantics=("arbitrary", "arbitrary"),
                                             vmem_limit_bytes=VMEM_LIMIT),
        name="mixer_inproj",
    )(*args)


def _attn_prompt_body(q_ref, k2_ref, kmt_ref, vt_ref, o_ref, va_ref, qh_ref, pen_ref, p_ref,
                      *, seq, nb, hd):
    blk = MOBA_BLOCK
    nbp = SUBLANES
    q2 = q_ref[0]
    ones_rows = (lax.broadcasted_iota(jnp.int32, (SUBLANES, seq), 0) == 0).astype(F32)
    for hh in range(2):
        va = jnp.concatenate([vt_ref[0, hh * hd:(hh + 1) * hd, :], ones_rows, jnp.zeros((SUBLANES, seq), F32)], axis=0)
        va_ref[hh] = va.astype(BF16)
    km = jnp.transpose(kmt_ref[0])
    lane_q = lax.broadcasted_iota(jnp.int32, (seq, 2 * hd), 1)
    lane_m = lax.broadcasted_iota(jnp.int32, (2 * SUBLANES, 2 * hd), 1)
    tpos = lax.broadcasted_iota(jnp.int32, (nbp, seq), 1)
    bidx = lax.broadcasted_iota(jnp.int32, (nbp, seq), 0)
    qblk = jnp.zeros((nbp, seq), jnp.int32)
    for b in range(1, nb):
        qblk = qblk + (tpos >= b * blk).astype(jnp.int32)
    for hh in range(2):
        in_head = (lane_m >= hh * hd) & (lane_m < (hh + 1) * hd)
        kmh = jnp.where(in_head, km[0:2 * SUBLANES], 0.0).astype(BF16)
        sct = _dot_nt(kmh, q2)[0:nbp]
        rank = jnp.zeros((nbp, seq), F32)
        for bp in range(nb):
            rowv = sct[bp:bp + 1, :]
            beats = (rowv > sct) | ((rowv == sct) & (bp < bidx))
            rank = rank + jnp.where(beats & (bp < qblk), 1.0, 0.0)
        keep = (bidx < qblk) & (rank < MOBA_TOPK)
        pen_ref[hh] = jnp.where(keep, 0.0, NEG)
        own_lanes = (lane_q >= hh * hd) & (lane_q < (hh + 1) * hd)
        qh_ref[hh] = jnp.where(own_lanes, q2, jnp.zeros_like(q2))

    kr = lax.broadcasted_iota(jnp.int32, (blk, blk), 0)
    qc = lax.broadcasted_iota(jnp.int32, (blk, blk), 1)
    causal = kr <= qc
    items = [(i, hh) for i in range(nb) for hh in range(2)]
    m_of, zero_of, o_of, s_of = {}, {}, {}, {}

    def base_of(i):
        return blk * (i * (i + 1) // 2)

    def logits_piece(k, j):
        i, hh = items[k]
        qs = slice(i * blk, (i + 1) * blk)
        sv = _dot_nt(k2_ref[0, j * blk:(j + 1) * blk, :], qh_ref[hh, qs, :])
        s_of[k, j] = sv
        if j == i:
            mj = jnp.max(jnp.where(causal, sv, NEG), axis=0, keepdims=True)
        else:
            mj = jnp.max(sv, axis=0, keepdims=True) + pen_ref[hh, j:j + 1, qs]
        m_of[k] = mj if j == 0 else jnp.maximum(m_of[k], mj)

    def exp_piece(k, j):
        i, hh = items[k]
        qs = slice(i * blk, (i + 1) * blk)
        if j == 0:
            zero_of[k] = [jnp.zeros((1, blk), F32)] * _EXP_DEPTH
        shift = m_of[k] if j == i else m_of[k] - pen_ref[hh, j:j + 1, qs]
        sv = s_of.pop((k, j))
        for c in range(blk // _EXP_ROWS):
            r0 = base_of(i) + j * blk + c * _EXP_ROWS
            x = sv[c * _EXP_ROWS:(c + 1) * _EXP_ROWS]
            if j == i:
                x = jnp.where(causal[c * _EXP_ROWS:(c + 1) * _EXP_ROWS], x, NEG)
            p = jnp.exp2(x - (shift + zero_of[k][c % _EXP_DEPTH]))
            p_ref[hh, r0:r0 + _EXP_ROWS, :] = p.astype(BF16)
            zero_of[k][c % _EXP_DEPTH] = jnp.minimum(p[_EXP_ROWS - 1:_EXP_ROWS], 0.0)

    def values(k):
        i, hh = items[k]
        ln = (i + 1) * blk
        ot = _dot(va_ref[hh, :, 0:ln], p_ref[hh, base_of(i):base_of(i) + ln, :])
        o_of[k] = ot[0:hd] / ot[hd:hd + 1]
        if hh == 1:
            both = jnp.concatenate([o_of.pop(k - 1), o_of.pop(k)], axis=0)
            o_ref[0, i * blk:(i + 1) * blk, :] = jnp.transpose(both).astype(BF16)

    for j in range(items[0][0] + 1):
        logits_piece(0, j)
    for k in range(len(items)):
        ahead = [(k + 1, j) for j in range(items[k + 1][0] + 1)] if k + 1 < len(items) else []
        for j in range(items[k][0] + 1):
            exp_piece(k, j)
            if ahead:
                logits_piece(*ahead.pop(0))
        for kj in ahead:
            logits_piece(*kj)
        values(k)


def _attn_prompt_call(q, k2, kmt, vt, *, n_heads, hd):
    n, s, d_att = q.shape
    nb = s // MOBA_BLOCK
    body = functools.partial(_attn_prompt_body, seq=s, nb=nb, hd=hd)
    tri = MOBA_BLOCK * (nb * (nb + 1) // 2)
    tok_spec = pl.BlockSpec((1, s, 2 * hd), lambda i, p: (i, 0, p))
    return pl.pallas_call(
        body, grid=(n, n_heads // 2),
        in_specs=[tok_spec, tok_spec,
                  pl.BlockSpec((1, 2 * hd, LANES), lambda i, p: (i, p, 0)),
                  pl.BlockSpec((1, 2 * hd, s), lambda i, p: (i, p, 0))],
        out_specs=tok_spec,
        out_shape=jax.ShapeDtypeStruct((n, s, d_att), BF16),
        scratch_shapes=[pltpu.VMEM((2, hd + 2 * SUBLANES, s), BF16),
                        pltpu.VMEM((2, s, 2 * hd), BF16), pltpu.VMEM((2, SUBLANES, s), F32),
                        pltpu.VMEM((2, tri, MOBA_BLOCK), BF16)],
        compiler_params=pltpu.CompilerParams(dimension_semantics=("arbitrary", "arbitrary"),
                                             vmem_limit_bytes=VMEM_LIMIT),
        name="moba_prompt",
    )(q, k2, kmt, vt)


def _post_body(x_ref, o_ref, ag_ref, gb_ref, mod_ref, wba_ref, wout_ref, gffn_ref, wup_ref, wconv_ref, bconv_ref,
               wdown_ref, cst_ref, y_ref, conv_ref, gbuf, act_ref, *, tm, rows_per_tok, halo, nt, pc, down_groups, d_ff):
    t = pl.program_id(1)
    x = x_ref[0]
    gt_m = mod_ref[0, 2]
    sh_f = mod_ref[0, 3]
    sc_f = mod_ref[0, 4]
    gt_f = mod_ref[0, 5]
    oa = _dot(o_ref[0], wba_ref[...])
    merged = ag_ref[0].astype(F32) + gb_ref[0].astype(F32) * oa
    x1 = x + gt_m * _dot(merged.astype(BF16), wout_ref[...])
    ms = jnp.mean(x1 * x1, axis=-1, keepdims=True)
    h2 = x1 * lax.rsqrt(ms + EPS) * gffn_ref[...]
    h2 = (h2 * (1.0 + sc_f) + sh_f).astype(BF16)

    @pl.when(t == 0)
    def _():
        gbuf[0:halo, :] = cst_ref[0]

    @pl.when(t > 0)
    def _():
        gbuf[0:halo, :] = gbuf[tm:tm + halo, :]

    r = rows_per_tok
    n_pc = d_ff // pc

    def up(c):
        cs = slice(c * pc, (c + 1) * pc)
        fg = _dot(h2, wup_ref[:, cs])
        fv = _dot(h2, wup_ref[:, d_ff + c * pc:d_ff + (c + 1) * pc])
        gbuf[halo:halo + tm, cs] = fg
        return fg, fv

    def activate(c, fg, fv):
        cs = slice(c * pc, (c + 1) * pc)
        conv = bconv_ref[:, cs]
        conv = conv + gbuf[halo - 2 * r:halo - 2 * r + tm, cs] * wconv_ref[0:1, cs]
        conv = conv + gbuf[halo - r:halo - r + tm, cs] * wconv_ref[1:2, cs]
        conv = conv + fg * wconv_ref[2:3, cs]
        act_ref[:, cs] = (conv * _sigmoid(conv) * fv).astype(BF16)

    bounds = [round(g * n_pc / down_groups) * pc for g in range(down_groups + 1)]
    acc = None
    nxt = up(0)
    for c in range(n_pc):
        cur = nxt
        if c + 1 < n_pc:
            nxt = up(c + 1)
        activate(c, *cur)
        if (c + 1) * pc in bounds[1:]:
            lo = bounds[bounds.index((c + 1) * pc) - 1]
            part = _dot(act_ref[:, lo:(c + 1) * pc], wdown_ref[lo:(c + 1) * pc, :])
            acc = part if acc is None else acc + part
    y_ref[0] = x1 + gt_f * acc

    @pl.when(t == nt - 1)
    def _():
        keep = (CONV_W - 1) * r
        conv_ref[0] = gbuf[halo + tm - keep:halo + tm, :]


def _post_call(x, o, ag, gb, mod, cstate, wts, *, tm, rows_per_tok, halo, pc, down_groups):
    n, s, d = x.shape
    nt = s // tm
    d_att = o.shape[-1]
    d_ff = wts["w_down"].shape[0]
    keep = (CONV_W - 1) * rows_per_tok
    rm = mod.shape[2]
    body = functools.partial(_post_body, tm=tm, rows_per_tok=rows_per_tok, halo=halo, nt=nt, pc=pc,
                             down_groups=down_groups, d_ff=d_ff)
    row_spec = lambda w: pl.BlockSpec((1, tm, w), lambda i, t: (i, t, 0))
    in_specs = [
        row_spec(d), row_spec(d_att), row_spec(d), row_spec(d),
        pl.BlockSpec((1, N_MOD, rm, d), lambda i, t: (i, 0, 0, 0)),
        _const_spec(wts["w_ba"].shape), _const_spec(wts["w_out"].shape), _const_spec((1, d)),
        _const_spec(wts["w_up"].shape), _const_spec((CONV_W, d_ff)), _const_spec((1, d_ff)),
        _const_spec(wts["w_down"].shape),
        pl.BlockSpec((1, halo, d_ff), lambda i, t: (i, 0, 0)),
    ]
    out_specs = [row_spec(d), pl.BlockSpec((1, keep, d_ff), lambda i, t: (i, 0, 0))]
    out_shape = [jax.ShapeDtypeStruct((n, s, d), F32), jax.ShapeDtypeStruct((n, keep, d_ff), F32)]
    return pl.pallas_call(
        body, grid=(n, nt), in_specs=in_specs, out_specs=out_specs, out_shape=out_shape,
        scratch_shapes=[pltpu.VMEM((halo + tm, d_ff), F32), pltpu.VMEM((tm, d_ff), BF16)],
        compiler_params=pltpu.CompilerParams(dimension_semantics=("arbitrary", "arbitrary"),
                                             vmem_limit_bytes=VMEM_LIMIT),
        name="merge_ffn",
    )(x, o, ag, gb, mod, wts["w_ba"], wts["w_out"], wts["g_ffn"], wts["w_up"], wts["w_conv"], wts["b_conv"],
      wts["w_down"], cstate)


def _kmean_body(pt_ref, ck_ref, km_ref, buf, sem, *, n_seq, n_pages, ppb):
    n = pl.program_id(0)
    rows = km_ref.shape[1]
    bpg = _KM_GROUP // ppb
    n_groups = n_pages // _KM_GROUP

    def copy(seq, p, slot):
        return pltpu.make_async_copy(ck_ref.at[pt_ref[seq, p]], buf.at[slot], sem.at[slot])

    def start_group(seq, g, half):
        for k in range(_KM_GROUP):
            copy(seq, g * _KM_GROUP + k, half * _KM_GROUP + k).start()

    @pl.when(n == 0)
    def _():
        start_group(0, 0, 0)
        start_group(0, 1, 1)

    km_ref[0] = jnp.zeros((rows, LANES), F32)
    lane = lax.broadcasted_iota(jnp.int32, (_KM_ROWS, LANES), 1)

    def pair(g2, carry):
        for half in range(2):
            g = g2 * 2 + half
            for k in range(_KM_GROUP):
                copy(n, g * _KM_GROUP + k, half * _KM_GROUP + k).wait()
            for rc in range(rows // _KM_ROWS):
                rs = slice(rc * _KM_ROWS, (rc + 1) * _KM_ROWS)
                upd = jnp.zeros((_KM_ROWS, LANES), F32)
                for bb in range(bpg):
                    slot = half * _KM_GROUP + bb * ppb
                    x = buf[slot, rs, :]
                    for pp in range(1, ppb):
                        x = x + buf[slot + pp, rs, :]
                    col = jnp.sum(x, axis=-1, keepdims=True)
                    upd = jnp.where(lane == g * bpg + bb, col, upd)
                km_ref[0, rs, :] = km_ref[0, rs, :] + upd * (1.0 / MOBA_BLOCK)
            nxt = g + 2

            @pl.when(nxt < n_groups)
            def _():
                start_group(n, nxt, half)

            @pl.when((nxt >= n_groups) & (n + 1 < n_seq))
            def _():
                start_group(n + 1, nxt - n_groups, half)
        return carry

    lax.fori_loop(0, n_groups // 2, pair, 0)


def _kmean_call(page_table, ck_t):
    n, n_pages = page_table.shape
    _, rows, page = ck_t.shape
    ppb = MOBA_BLOCK // page
    assert n_pages % (2 * _KM_GROUP) == 0 and rows % _KM_ROWS == 0
    body = functools.partial(_kmean_body, n_seq=n, n_pages=n_pages, ppb=ppb)
    grid_spec = pltpu.PrefetchScalarGridSpec(
        num_scalar_prefetch=1, grid=(n,),
        in_specs=[pl.BlockSpec(memory_space=pl.ANY)],
        out_specs=pl.BlockSpec((1, rows, LANES), lambda i, pt: (i, 0, 0)),
        scratch_shapes=[pltpu.VMEM((2 * _KM_GROUP, rows, page), F32), pltpu.SemaphoreType.DMA((2 * _KM_GROUP,))],
    )
    return pl.pallas_call(
        body, grid_spec=grid_spec,
        out_shape=jax.ShapeDtypeStruct((n, rows, LANES), F32),
        compiler_params=pltpu.CompilerParams(dimension_semantics=("arbitrary",), vmem_limit_bytes=VMEM_LIMIT),
        name="cache_block_means",
    )(page_table, ck_t)


def _topk_body(q_ref, km_ref, sel_ref, *, n_heads, hd, n_blk, top, unroll):
    n_seq = q_ref.shape[0]
    rows = n_heads * SUBLANES
    lane_q = lax.broadcasted_iota(jnp.int32, (SUBLANES, n_heads * hd), 1)
    lane = lax.broadcasted_iota(jnp.int32, (unroll * rows, LANES), 1).astype(F32)

    def per_group(ig, carry):
        scs = []
        for k in range(unroll):
            i = ig * unroll + k
            q8 = q_ref[i]
            qbd = jnp.concatenate(
                [jnp.where((lane_q >= hh * hd) & (lane_q < (hh + 1) * hd), q8, 0.0) for hh in range(n_heads)], axis=0)
            scs.append(_dot(qbd.astype(BF16), km_ref[i].astype(BF16)))
        sc = jnp.where(lane < n_blk, jnp.concatenate(scs, axis=0), -jnp.inf)
        out = jnp.zeros((unroll * rows, LANES), F32)
        for r in range(top):
            m = jnp.max(sc, axis=-1, keepdims=True)
            idx = jnp.min(jnp.where(sc == m, lane, float(LANES)), axis=-1, keepdims=True)
            out = jnp.where(lane == r, idx, out)
            sc = jnp.where(lane == idx, -jnp.inf, sc)
        out = out.astype(jnp.int32)
        for k in range(unroll):
            sel_ref[ig * unroll + k] = out[k * rows:(k + 1) * rows]
        return carry

    lax.fori_loop(0, n_seq // unroll, per_group, 0)


def _topk_call(q8, kmt, *, n_heads, hd, n_blk, top):
    n = q8.shape[0]
    rows = n_heads * SUBLANES
    unroll = max(u for u in (8, 4, 2, 1) if n % u == 0)
    body = functools.partial(_topk_body, n_heads=n_heads, hd=hd, n_blk=n_blk, top=top, unroll=unroll)
    return pl.pallas_call(
        body, grid=(1,),
        in_specs=[pl.BlockSpec(q8.shape, lambda i: (0, 0, 0)), pl.BlockSpec(kmt.shape, lambda i: (0, 0, 0))],
        out_specs=pl.BlockSpec((n, rows, LANES), lambda i: (0, 0, 0)),
        out_shape=jax.ShapeDtypeStruct((n, rows, LANES), jnp.int32),
        compiler_params=pltpu.CompilerParams(vmem_limit_bytes=VMEM_LIMIT),
        name="block_topk",
    )(q8, kmt)


def _attn_sample_body(pt_ref, sel_ref, qt_ref, knt_ref, vnt_ref, lseq_ref, ltok_ref, ck_ref, cv_ref, o_ref, kbuf, vbuf,
                      lg_ref, p_ref, sem,
                      *, n_seq, n_heads, hd, n_tok, top, ppb):
    n = pl.program_id(0)
    n_sel = top * ppb
    n_grp = n_heads * n_tok
    slot = lax.rem(n, 2)

    def gather(seq, sl):
        for hh in range(n_heads):
            for s in range(n_tok):
                g = hh * n_tok + s
                for j in range(n_sel):
                    r, pp = divmod(j, ppb)
                    page = pt_ref[seq, sel_ref[seq, g * top + r] * ppb + pp]
                    pltpu.make_async_copy(ck_ref.at[page, hh], kbuf.at[sl, g, j], sem.at[sl]).start()
                    pltpu.make_async_copy(cv_ref.at[page, hh], vbuf.at[sl, g, j], sem.at[sl]).start()

    @pl.when(n == 0)
    def _():
        gather(0, 0)

    @pl.when(n + 1 < n_seq)
    def _():
        gather(n + 1, 1 - slot)

    for g in range(n_grp):
        for j in range(n_sel):
            pltpu.make_async_copy(ck_ref.at[0, 0], kbuf.at[slot, g, j], sem.at[slot]).wait()
            pltpu.make_async_copy(cv_ref.at[0, 0], vbuf.at[slot, g, j], sem.at[slot]).wait()

    mine = lseq_ref[...] == n
    lane_tok = ltok_ref[...]
    pad_rows = [jnp.full((1, LANES), NEG, F32)] * (SUBLANES - n_sel - 1)
    for hh in range(n_heads):
        rs = slice(hh * hd, (hh + 1) * hd)
        knt = knt_ref[rs, :]
        for s in range(n_tok):
            g = hh * n_tok + s
            qb = jnp.broadcast_to(qt_ref[0, rs, s:s + 1], (hd, LANES))
            rows = [jnp.sum(kbuf[slot, g, j] * qb, axis=0, keepdims=True) for j in range(n_sel)]
            own = jnp.sum(knt * qb, axis=0, keepdims=True)
            rows.append(jnp.where(mine & (lane_tok <= s), own, NEG))
            lg_ref[g] = jnp.concatenate(rows + pad_rows, axis=0)

    lg = lg_ref[...]
    m = jnp.max(jnp.max(lg, axis=1), axis=-1, keepdims=True)
    mb = jnp.broadcast_to(m, (n_grp, LANES))
    for g in range(n_grp):
        p_ref[g] = jnp.exp2(lg_ref[g] - mb[g:g + 1, :])
    l = jnp.sum(jnp.sum(p_ref[...], axis=1), axis=-1, keepdims=True)
    inv = jnp.broadcast_to(1.0 / l, (n_grp, LANES))

    lane_o = lax.broadcasted_iota(jnp.int32, (hd, LANES), 1)
    outs = []
    for hh in range(n_heads):
        rs = slice(hh * hd, (hh + 1) * hd)
        vnt = vnt_ref[rs, :]
        out_h = jnp.zeros((hd, LANES), F32)
        for s in range(n_tok):
            g = hh * n_tok + s
            pg = p_ref[g] * inv[g:g + 1, :]
            acc = vnt * pg[n_sel:n_sel + 1, :]
            for j in range(n_sel):
                acc = acc + vbuf[slot, g, j] * pg[j:j + 1, :]
            col = jnp.sum(acc, axis=-1, keepdims=True)
            out_h = jnp.where(lane_o == s, col, out_h)
        outs.append(out_h)
    o_ref[0] = jnp.transpose(jnp.concatenate(outs, axis=0))[0:SUBLANES]


def _attn_sample_call(page_table, sel, qt, knt, vnt, lane_seq, lane_tok, ck_t, cv_t, *, n_heads, hd, n_tok, top):
    n = page_table.shape[0]
    page = ck_t.shape[-1]
    ppb = MOBA_BLOCK // page
    d_att = n_heads * hd
    n_grp = n_heads * n_tok
    assert top * ppb + 1 <= SUBLANES
    body = functools.partial(_attn_sample_body, n_seq=n, n_heads=n_heads, hd=hd, n_tok=n_tok, top=top, ppb=ppb)
    blk = pl.BlockSpec((1, d_att, LANES), lambda i, pt, sl: (i, 0, 0))
    whole = lambda shape: pl.BlockSpec(shape, lambda i, pt, sl: (0,) * len(shape))
    grid_spec = pltpu.PrefetchScalarGridSpec(
        num_scalar_prefetch=2, grid=(n,),
        in_specs=[blk, whole((d_att, LANES)), whole((d_att, LANES)), whole((1, LANES)), whole((1, LANES)),
                  pl.BlockSpec(memory_space=pl.ANY), pl.BlockSpec(memory_space=pl.ANY)],
        out_specs=pl.BlockSpec((1, SUBLANES, d_att), lambda i, pt, sl: (i, 0, 0)),
        scratch_shapes=[pltpu.VMEM((2, n_grp, top * ppb, hd, page), F32),
                        pltpu.VMEM((2, n_grp, top * ppb, hd, page), F32),
                        pltpu.VMEM((n_grp, SUBLANES, LANES), F32),
                        pltpu.VMEM((n_grp, SUBLANES, LANES), F32),
                        pltpu.SemaphoreType.DMA((2,))],
    )
    return pl.pallas_call(
        body, grid_spec=grid_spec,
        out_shape=jax.ShapeDtypeStruct((n, SUBLANES, d_att), F32),
        compiler_params=pltpu.CompilerParams(dimension_semantics=("arbitrary",), vmem_limit_bytes=VMEM_LIMIT),
        name="moba_sample",
    )(page_table, sel, qt, knt, vnt, lane_seq, lane_tok, ck_t, cv_t)


def _layer_weights(l, g_norm_mix, w_in, g_q, g_k, w_pool_group, pool_scale, w_branch_pool, w_branch_attn, w_out,
                   g_norm_ffn, w_up, w_conv, b_conv, w_down, n_heads, hd):
    d = w_in.shape[1]
    pw = w_pool_group.shape[1] * w_pool_group.shape[2]
    d_att = n_heads * hd
    wi = w_in[l].astype(BF16)
    c1, c2, c3 = pw + d_att, pw + 2 * d_att, pw + 3 * d_att
    blk = jnp.arange(MXU_DIM) // hd
    return {
        "g_mix": g_norm_mix[l].reshape(1, d),
        "w_in": wi, "w_kt": wi[:, c1:c2].T, "w_vt": wi[:, c2:c3].T,
        "g_q": g_q[l].reshape(1, d_att), "g_k": g_k[l].reshape(d_att, 1),
        "bd": jnp.where(blk[:, None] == blk[None, :], 1.0 / hd, 0.0).astype(BF16),
        "w_pg": w_pool_group[l].astype(BF16), "p_scale": pool_scale[l].reshape(1, pw),
        "w_bp": w_branch_pool[l].astype(BF16),
        "w_ba": w_branch_attn[l].astype(BF16), "w_out": w_out[l].astype(BF16),
        "g_ffn": g_norm_ffn[l].reshape(1, d), "w_up": w_up[l].astype(BF16),
        "w_conv": w_conv[l], "b_conv": b_conv[l].reshape(1, -1), "w_down": w_down[l].astype(BF16),
    }


def kernel(x_prompt, x_sample, cache_k, cache_v, state_pool, state_ffn_conv, page_table, c_prompt, c_sample, w_ada, b_ada, g_norm_mix, w_in, g_q, g_k, w_pool_group, pool_scale, w_branch_pool, w_branch_attn, w_out, g_norm_ffn, w_up, w_conv, b_conv, w_down):
    nb_p, seq, d = x_prompt.shape
    nb_s, n_tok, _ = x_sample.shape
    depth, n_phys, page, n_heads, hd = cache_k.shape
    d_att = n_heads * hd
    pw = state_pool.shape[-1]
    d_ff = w_down.shape[1]
    n_pages = page_table.shape[1]
    past = n_pages * page
    assert MOBA_BLOCK % page == 0 and past % MOBA_BLOCK == 0 and seq % MOBA_BLOCK == 0
    assert n_tok <= SUBLANES and nb_s * n_tok <= LANES
    n_blk = past // MOBA_BLOCK
    top = min(MOBA_TOPK, n_blk)
    assert 0 < n_blk <= LANES and seq // MOBA_BLOCK <= SUBLANES
    tm = 256
    tm_ffn = 512 if seq % 512 == 0 else tm
    pc, dg = MXU_DIM, 3
    assert d_ff % pc == 0
    rows_s = nb_s * n_tok

    y_p, y_s = x_prompt, x_sample.transpose(1, 0, 2).reshape(1, rows_s, d)
    halo_p = 4 * SUBLANES
    zero_pool = jnp.zeros((nb_p, halo_p, pw), F32)
    zero_conv = jnp.zeros((nb_p, SUBLANES, d_ff), F32)
    outs = [[] for _ in range(8)]
    for l in range(depth):
        wts = _layer_weights(l, g_norm_mix, w_in, g_q, g_k, w_pool_group, pool_scale, w_branch_pool, w_branch_attn,
                             w_out, g_norm_ffn, w_up, w_conv, b_conv, w_down, n_heads, hd)
        mod = _ada_call(jnp.concatenate([c_prompt, c_sample], axis=0), w_ada[l].astype(BF16), b_ada[l])
        mod_p = mod[:nb_p].reshape(nb_p, N_MOD, 1, d)
        mod_s = jnp.tile(mod[nb_p:].reshape(nb_s, N_MOD, d), (n_tok, 1, 1)).transpose(1, 0, 2)[None]

        ck_t = cache_k[l].transpose(0, 2, 3, 1)
        cv_t = cache_v[l].transpose(0, 2, 3, 1)
        ck_flat = ck_t.reshape(n_phys, d_att, page)
        gpsq, steps = n_pages // _KM_GROUP, nb_p * (seq // tm)
        gps = nb_s * gpsq // steps
        fuse = n_pages % _KM_GROUP == 0 and gps >= 1 and gps * steps == nb_s * gpsq and gpsq % gps == 0
        res = _inproj_call(y_p, mod_p, zero_pool, wts, tm=tm, rows_per_tok=1, halo=halo_p, pos0=0,
                           n_heads=n_heads, hd=hd, block_means=True, pages=(page_table, ck_flat) if fuse else None)
        kt, vt, q, k2, kmt, ag, gb, pool_p = res[:8]
        kmt_s = res[8] if fuse else _kmean_call(page_table, ck_flat)
        o = _attn_prompt_call(q, k2, kmt, vt, n_heads=n_heads, hd=hd)
        y_p, conv_p = _post_call(y_p, o, ag, gb, mod_p, zero_conv, wts, tm=tm_ffn, rows_per_tok=1, halo=SUBLANES, pc=pc,
                                 down_groups=dg)
        to_rows = lambda a: a.reshape(nb_p, n_heads, hd, seq).transpose(0, 3, 1, 2)
        outs[0].append(to_rows(kt)); outs[1].append(to_rows(vt)); outs[2].append(pool_p); outs[3].append(conv_p)

        pool_state = state_pool[l].transpose(1, 0, 2).reshape(1, POOL_CTX * nb_s, pw)
        conv_state = state_ffn_conv[l].transpose(1, 0, 2).reshape(1, (CONV_W - 1) * nb_s, d_ff)
        kt_s, vt_s, q_s, _, _, ag_s, gb_s, pool_s = _inproj_call(
            y_s, mod_s, pool_state, wts, tm=rows_s, rows_per_tok=nb_s, halo=POOL_CTX * nb_s, pos0=past,
            n_heads=n_heads, hd=hd, block_means=False)
        q_ns = q_s.reshape(n_tok, nb_s, d_att).transpose(1, 0, 2).astype(F32)
        q8 = jnp.pad(q_ns, ((0, 0), (0, SUBLANES - n_tok), (0, 0)))
        sel = _topk_call(q8, kmt_s, n_heads=n_heads, hd=hd, n_blk=n_blk, top=top)
        sel = sel.reshape(nb_s, n_heads, SUBLANES, LANES)[:, :, :n_tok, :top].reshape(nb_s, n_heads * n_tok * top)
        qt = jnp.pad(q_ns.transpose(0, 2, 1), ((0, 0), (0, 0), (0, LANES - n_tok)))
        lane_pad = lambda a: jnp.pad(a, ((0, 0), (0, LANES - rows_s)))
        lanes = jnp.arange(LANES)
        lane_seq = jnp.where(lanes < rows_s, lanes % nb_s, -1).astype(jnp.int32).reshape(1, LANES)
        lane_tok = (lanes // nb_s).astype(jnp.int32).reshape(1, LANES)
        ot = _attn_sample_call(page_table, sel, qt, lane_pad(kt_s[0]), lane_pad(vt_s[0]), lane_seq, lane_tok, ck_t, cv_t,
                               n_heads=n_heads, hd=hd, n_tok=n_tok, top=top)
        o_s = ot[:, :n_tok].transpose(1, 0, 2).reshape(1, rows_s, d_att).astype(BF16)
        y_s, conv_s = _post_call(y_s, o_s, ag_s, gb_s, mod_s, conv_state, wts, tm=rows_s, rows_per_tok=nb_s,
                                 halo=(CONV_W - 1) * nb_s, pc=pc, down_groups=dg)
        to_rows_s = lambda a: a[0].reshape(n_heads, hd, n_tok, nb_s).transpose(3, 2, 0, 1)
        outs[4].append(to_rows_s(kt_s)); outs[5].append(to_rows_s(vt_s))
        outs[6].append(pool_s.reshape(POOL_CTX, nb_s, pw).transpose(1, 0, 2))
        outs[7].append(conv_s.reshape(CONV_W - 1, nb_s, d_ff).transpose(1, 0, 2))

    y_s_out = y_s.reshape(n_tok, nb_s, d).transpose(1, 0, 2)
    st = [jnp.stack(v) for v in outs]
    return (y_p, y_s_out, st[0], st[1], st[2], st[3], st[4], st[5], st[6], st[7])
```

```python
import functools

import jax
import jax.numpy as jnp
from jax import lax
from jax.experimental import pallas as pl
from jax.experimental.pallas import tpu as pltpu

F32 = jnp.float32
BF16 = jnp.bfloat16

MOBA_BLOCK = 256
MOBA_TOPK = 3
POOL_WINDOWS = (2, 4, 8, 16)
POOL_CTX = max(POOL_WINDOWS) - 1
CONV_W = 3
N_MOD = 6
EPS = 1e-6
NEG = -1e30
LOG2E = 1.4426950408889634

LANES = 128
SUBLANES = 8
MXU_DIM = 256
VMEM_LIMIT = 60 * 1024 * 1024
_EXP_ROWS = 64
_EXP_DEPTH = 1
_KM_GROUP = 16
_KM_ROWS = 64


def _const_spec(shape):
    nd = len(shape)
    return pl.BlockSpec(shape, lambda *_: (0,) * nd, pipeline_mode=pl.Buffered(1))


def _dot(a, b):
    return jnp.dot(a, b, preferred_element_type=F32)


def _dot_nt(a, b):
    return lax.dot_general(a, b, (((1,), (1,)), ((), ())), preferred_element_type=F32)


def _sigmoid(x):
    return 1.0 / (1.0 + jnp.exp(-x))


def _ada_body(c_ref, w_ref, b_ref, o_ref):
    c = c_ref[...]
    a = (c * _sigmoid(c)).astype(BF16)
    o_ref[...] = _dot(a, w_ref[...]) + b_ref[...]


def _ada_call(c, w_ada, b_ada):
    n, d = c.shape
    dm = w_ada.shape[1]
    bn = d
    return pl.pallas_call(
        _ada_body,
        grid=(dm // bn,),
        in_specs=[pl.BlockSpec((n, d), lambda j: (0, 0)),
                  pl.BlockSpec((d, bn), lambda j: (0, j)),
                  pl.BlockSpec((1, bn), lambda j: (0, j))],
        out_specs=pl.BlockSpec((n, bn), lambda j: (0, j)),
        out_shape=jax.ShapeDtypeStruct((n, dm), F32),
        name="ada_mod",
    )(c, w_ada, b_ada.reshape(1, dm))


def _inproj_body(*refs, tm, rows_per_tok, halo, pos0, nt, n_heads, hd, scale, block_means, stream):
    if stream is not None:
        pt_ref, refs = refs[0], refs[1:]
    (x_ref, mod_ref, gmix_ref, win_ref, gq_ref, gk_ref, bd_ref, wpg_ref, psc_ref,
     wbp_ref, st_ref) = refs[:11]
    refs = refs[11:]
    if stream is not None:
        ck_ref, refs = refs[0], refs[1:]
    kt_ref, vt_ref, q_ref, k2_ref, kmt_ref, ag_ref, gb_ref, pool_ref = refs[:8]
    refs = refs[8:]
    if stream is not None:
        kms_ref, ubuf, wbuf, wkv_t, pbuf, psem = refs
    else:
        ubuf, wbuf, wkv_t = refs
    n_lvl = max(len(POOL_WINDOWS) - 1, 1)
    t = pl.program_id(1)
    d_att = n_heads * hd
    d_model = x_ref.shape[-1]
    cw = MXU_DIM

    todo = []
    if stream is not None:
        gps, gpsq, total, ppb = stream
        bpg = _KM_GROUP // ppb
        step = pl.program_id(0) * nt + t
        n_steps = total // gps
        km_rows = kms_ref.shape[1]
        lane_s = lax.broadcasted_iota(jnp.int32, (_KM_ROWS, LANES), 1)
        side_set = lax.rem(step, 2)

        def batch_copies(b, sset):
            out = []
            for k in range(gps):
                gidx = b * gps + k
                seq, g = lax.div(gidx, gpsq), lax.rem(gidx, gpsq)
                for j in range(_KM_GROUP):
                    slot = (sset * gps + k) * _KM_GROUP + j
                    out.append(pltpu.make_async_copy(ck_ref.at[pt_ref[seq, g * _KM_GROUP + j]], pbuf.at[slot],
                                                     psem.at[sset]))
            return out

        @pl.when(step == 0)
        def _():
            for b0 in range(min(2, n_steps)):
                for cp in batch_copies(jnp.int32(b0), b0):
                    cp.start()

        for slot in range(gps * _KM_GROUP):
            pltpu.make_async_copy(ck_ref.at[0], pbuf.at[side_set * gps * _KM_GROUP + slot], psem.at[side_set]).wait()

        @pl.when(lax.rem(step * gps, gpsq) == 0)
        def _():
            kms_ref[0] = jnp.zeros((km_rows, LANES), F32)

        def stream_reduce(k, rc):
            g = lax.rem(step * gps + k, gpsq)
            rs = slice(rc * _KM_ROWS, (rc + 1) * _KM_ROWS)
            upd = jnp.zeros((_KM_ROWS, LANES), F32)
            for bb in range(bpg):
                slot = (side_set * gps + k) * _KM_GROUP + bb * ppb
                xs = pbuf[slot, rs, :]
                for pp in range(1, ppb):
                    xs = xs + pbuf[slot + pp, rs, :]
                upd = jnp.where(lane_s == g * bpg + bb, jnp.sum(xs, axis=-1, keepdims=True), upd)
            kms_ref[0, rs, :] = kms_ref[0, rs, :] + upd * (1.0 / MOBA_BLOCK)

        todo = [(k, rc) for k in range(gps) for rc in range(km_rows // _KM_ROWS)]

    def side(n_pieces=1):
        for _ in range(n_pieces):
            if todo:
                stream_reduce(*todo.pop(0))

    pw = wpg_ref.shape[0] * wpg_ref.shape[2]
    q0, k0, v0, g0 = pw, pw + d_att, pw + 2 * d_att, pw + 3 * d_att

    @pl.when((pl.program_id(0) == 0) & (t == 0))
    def _():
        wkv_t[0] = jnp.transpose(win_ref[:, k0:v0].astype(F32)).astype(BF16)
        wkv_t[1] = jnp.transpose(win_ref[:, v0:g0].astype(F32)).astype(BF16)

    x = x_ref[0]
    sh_m = mod_ref[0, 0]
    sc_m = mod_ref[0, 1]
    ms = jnp.mean(x * x, axis=-1, keepdims=True)
    h = x * lax.rsqrt(ms + EPS) * gmix_ref[...]
    h = (h * (1.0 + sc_m) + sh_m).astype(BF16)

    u = jnp.concatenate([_dot(h, win_ref[:, c * cw:(c + 1) * cw]) for c in range(pw // cw)], axis=1)

    @pl.when(t == 0)
    def _():
        ubuf[0:halo, :] = st_ref[0]

    @pl.when(t > 0)
    def _():
        ubuf[0:halo, :] = ubuf[tm:tm + halo, :]

    ubuf[halo:halo + tm, :] = u
    gw = LANES

    def window(g):
        w = POOL_WINDOWS[g]
        cs = slice(g * gw, (g + 1) * gw)
        n_rows = halo + tm
        src, valid_from, lvl, d = None, 0, 0, 1
        while 2 * d < w:
            s = d * rows_per_tok
            a = -(-s // SUBLANES) * SUBLANES
            dst = wbuf.at[g * n_lvl + lvl]
            if src is None:
                dst[a:n_rows, :] = ubuf[a:n_rows, cs] + ubuf[a - s:n_rows - s, cs]
            else:
                dst[a:n_rows, :] = src[a:n_rows, :] + src[a - s:n_rows - s, :]
            valid_from = max(a, valid_from + s)
            src, lvl, d = dst, lvl + 1, 2 * d
        s = d * rows_per_tok
        assert halo - s >= valid_from
        if src is None:
            acc = ubuf[halo:n_rows, cs] + ubuf[halo - s:n_rows - s, cs]
        else:
            acc = src[halo:n_rows, :] + src[halo - s:n_rows - s, :]
        if pos0 >= w - 1:
            cnt = float(w)
        else:
            assert rows_per_tok == 1
            row = lax.broadcasted_iota(jnp.int32, (tm, 1), 0) + (t * tm + pos0 + 1)
            cnt = jnp.minimum(float(w), row.astype(F32))
        return (acc / cnt - u[:, cs]).astype(BF16)

    def key_heads(kt_piece, c):
        outs = []
        for hh in range(cw // hd):
            rs = slice(c * cw + hh * hd, c * cw + (hh + 1) * hd)
            kh = kt_piece[hh * hd:(hh + 1) * hd]
            msk = jnp.sum(kh * kh, axis=0, keepdims=True) * (1.0 / hd)
            outs.append(kh * lax.rsqrt(msk + EPS) * gk_ref[rs])
            kt_ref[0, rs, :] = outs[-1]
        return jnp.concatenate(outs, axis=0)

    n_kc = d_att // cw
    pooled, kn = [], []
    for c in range(n_kc):
        ktc = _dot_nt(wkv_t[0, c * cw:(c + 1) * cw, :], h)
        pooled.append(window(c))
        side()
        kn.append((ktc, c))
    for c in range(n_kc):
        vt_ref[0, c * cw:(c + 1) * cw, :] = _dot_nt(wkv_t[1, c * cw:(c + 1) * cw, :], h)
        kn[c] = key_heads(*kn[c])
        side()
    kn = jnp.concatenate(kn, axis=0)
    qs = []
    for c in range(n_kc):
        qs.append(_dot(h, win_ref[:, q0 + c * cw:q0 + (c + 1) * cw]))
        if n_kc + c < len(POOL_WINDOWS):
            pooled.append(window(n_kc + c))
        side()
    for g in range(len(pooled), len(POOL_WINDOWS)):
        pooled.append(window(g))
    k2_ref[0] = jnp.transpose(kn).astype(BF16)

    @pl.when(t == 0)
    def _():
        kmt_ref[0] = jnp.zeros((d_att, LANES), F32)

    if block_means:
        lane_b = lax.broadcasted_iota(jnp.int32, (d_att, LANES), 1)
        for bb in range(tm // MOBA_BLOCK):
            part = kn[:, bb * MOBA_BLOCK:bb * MOBA_BLOCK + LANES]
            for c in range(1, MOBA_BLOCK // LANES):
                part = part + kn[:, bb * MOBA_BLOCK + c * LANES:bb * MOBA_BLOCK + (c + 1) * LANES]
            col = jnp.sum(part, axis=-1, keepdims=True) * (1.0 / MOBA_BLOCK)
            kmt_ref[0] = jnp.where(lane_b == t * (tm // MOBA_BLOCK) + bb, col, kmt_ref[0])

    ys = [_dot(pooled[g], wpg_ref[g]) for g in range(len(POOL_WINDOWS))]
    split = []
    for c in range(n_kc):
        qq = qs[c] * qs[c]
        hi = qq.astype(BF16)
        split.append((hi, (qq - hi.astype(F32)).astype(BF16)))
    side()
    y = (jnp.concatenate(ys, axis=1) * psc_ref[...]).astype(BF16)
    a = []
    for c in range(d_model // cw):
        a.append(_dot(y, wbp_ref[:, c * cw:(c + 1) * cw]))
        if c < n_kc:
            msq = _dot(split[c][0], bd_ref[...]) + _dot(split[c][1], bd_ref[...])
            cs = slice(c * cw, (c + 1) * cw)
            q_ref[0, :, cs] = (qs[c] * lax.rsqrt(msq + EPS) * gq_ref[:, cs] * scale).astype(BF16)
        side()

    n_gc = 2 * d_model // cw
    gate = [None] * n_gc

    def finish_gate(c):
        sg = _sigmoid(gate[c])
        lo_c = c * cw
        if lo_c < d_model:
            ag_ref[0, :, lo_c:lo_c + cw] = (sg * a[c]).astype(BF16)
        else:
            gb_ref[0, :, lo_c - d_model:lo_c - d_model + cw] = sg.astype(BF16)

    for c in range(n_gc):
        gate[c] = _dot(h, win_ref[:, g0 + c * cw:g0 + (c + 1) * cw])
        if c > 0:
            finish_gate(c - 1)
        side()
    finish_gate(n_gc - 1)
    side(len(todo))

    @pl.when(t == nt - 1)
    def _():
        keep = POOL_CTX * rows_per_tok
        pool_ref[0] = ubuf[halo + tm - keep:halo + tm, :]

    if stream is not None:
        @pl.when(step + 2 < n_steps)
        def _():
            for cp in batch_copies(step + 2, side_set):
                cp.start()


def _inproj_call(x, mod, state, wts, *, tm, rows_per_tok, halo, pos0, n_heads, hd, block_means, pages=None):
    n, s, d = x.shape
    nt = s // tm
    d_att = n_heads * hd
    pw = wts["p_scale"].shape[1]
    keep = POOL_CTX * rows_per_tok
    stream = None
    if pages is not None:
        page_table, ck_t = pages
        n_seq, n_pages = page_table.shape
        _, km_rows, page = ck_t.shape
        gpsq = n_pages // _KM_GROUP
        total = n_seq * gpsq
        gps = total // (n * nt)
        assert n_pages % _KM_GROUP == 0 and total == gps * n * nt and gps >= 1 and km_rows % _KM_ROWS == 0
        assert gpsq % gps == 0
        stream = (gps, gpsq, total, MOBA_BLOCK // page)
    body = functools.partial(_inproj_body, tm=tm, rows_per_tok=rows_per_tok, halo=halo, pos0=pos0, nt=nt,
                             n_heads=n_heads, hd=hd, scale=hd ** -0.5 * LOG2E, block_means=block_means, stream=stream)
    rm = mod.shape[2]

    def cs(shape):
        nd = len(shape)
        return pl.BlockSpec(shape, lambda *_: (0,) * nd, pipeline_mode=pl.Buffered(1))

    def bs(shape, fn):
        return pl.BlockSpec(shape, lambda i, t, *_: fn(i, t))

    in_specs = [
        bs((1, tm, d), lambda i, t: (i, t, 0)),
        bs((1, N_MOD, rm, d), lambda i, t: (i, 0, 0, 0)),
        cs((1, d)),
        cs(wts["w_in"].shape),
        cs((1, d_att)), cs((d_att, 1)), cs((MXU_DIM, MXU_DIM)),
        cs(wts["w_pg"].shape), cs((1, pw)), cs(wts["w_bp"].shape),
        bs((1, halo, pw), lambda i, t: (i, 0, 0)),
    ]
    out_specs = [
        bs((1, d_att, tm), lambda i, t: (i, 0, t)),
        bs((1, d_att, tm), lambda i, t: (i, 0, t)),
        bs((1, tm, d_att), lambda i, t: (i, t, 0)),
        bs((1, tm, d_att), lambda i, t: (i, t, 0)),
        bs((1, d_att, LANES), lambda i, t: (i, 0, 0)),
        bs((1, tm, d), lambda i, t: (i, t, 0)),
        bs((1, tm, d), lambda i, t: (i, t, 0)),
        bs((1, keep, pw), lambda i, t: (i, 0, 0)),
    ]
    out_shape = [
        jax.ShapeDtypeStruct((n, d_att, s), F32),
        jax.ShapeDtypeStruct((n, d_att, s), F32),
        jax.ShapeDtypeStruct((n, s, d_att), BF16),
        jax.ShapeDtypeStruct((n, s, d_att), BF16),
        jax.ShapeDtypeStruct((n, d_att, LANES), F32),
        jax.ShapeDtypeStruct((n, s, d), BF16),
        jax.ShapeDtypeStruct((n, s, d), BF16),
        jax.ShapeDtypeStruct((n, keep, pw), F32),
    ]
    n_lvl = max(len(POOL_WINDOWS) - 1, 1)
    scratch = [pltpu.VMEM((halo + tm, pw), F32), pltpu.VMEM((len(POOL_WINDOWS) * n_lvl, halo + tm, LANES), F32),
               pltpu.VMEM((2, d_att, d), BF16)]
    args = [x, mod, wts["g_mix"], wts["w_in"], wts["g_q"], wts["g_k"],
            wts["bd"], wts["w_pg"], wts["p_scale"], wts["w_bp"], state]
    n_prefetch = 0
    if stream is not None:
        gps, gpsq = stream[0], stream[1]
        in_specs.append(pl.BlockSpec(memory_space=pl.ANY))
        out_specs.append(bs((1, km_rows, LANES), lambda i, t: ((i * nt + t) * gps // gpsq, 0, 0)))
        out_shape.append(jax.ShapeDtypeStruct((n_seq, km_rows, LANES), F32))
        scratch += [pltpu.VMEM((2 * gps * _KM_GROUP, km_rows, page), F32), pltpu.SemaphoreType.DMA((2,))]
        args = [page_table] + args + [ck_t]
        n_prefetch = 1
    grid_spec = pltpu.PrefetchScalarGridSpec(num_scalar_prefetch=n_prefetch, grid=(n, nt), in_specs=in_specs,
                                             out_specs=out_specs, scratch_shapes=scratch)
    return pl.pallas_call(
        body, grid_spec=grid_spec, out_shape=out_shape,
        compiler_params=pltpu.CompilerParams(dimension_semantics=("arbitrary", "arbitrary"),
                                             vmem_limit_bytes=VMEM_LIMIT),
        name="mixer_inproj",
    )(*args)


def _attn_prompt_body(q_ref, k2_ref, kmt_ref, vt_ref, o_ref, va_ref, qh_ref, pen_ref, p_ref,
                      *, seq, nb, hd, n_pairs):
    blk = MOBA_BLOCK
    nbp = SUBLANES
    pw = 2 * hd
    lane_q = lax.broadcasted_iota(jnp.int32, (seq, pw), 1)
    lane_m = lax.broadcasted_iota(jnp.int32, (2 * SUBLANES, pw), 1)
    tpos = lax.broadcasted_iota(jnp.int32, (nbp, seq), 1)
    bidx = lax.broadcasted_iota(jnp.int32, (nbp, seq), 0)
    qblk = jnp.zeros((nbp, seq), jnp.int32)
    for b in range(1, nb):
        qblk = qblk + (tpos >= b * blk).astype(jnp.int32)
    ones_rows = (lax.broadcasted_iota(jnp.int32, (SUBLANES, seq), 0) == 0).astype(F32)
    kr = lax.broadcasted_iota(jnp.int32, (blk, blk), 0)
    qc = lax.broadcasted_iota(jnp.int32, (blk, blk), 1)
    causal = kr <= qc

    def setup_pieces(pair):
        ps = slice(pair * pw, (pair + 1) * pw)

        def values_piece(hh):
            rows = slice(pair * pw + hh * hd, pair * pw + (hh + 1) * hd)
            va = jnp.concatenate([vt_ref[0, rows, :], ones_rows, jnp.zeros((SUBLANES, seq), F32)], axis=0)
            va_ref[2 * pair + hh] = va.astype(BF16)

        def select_piece(hh):
            q2 = q_ref[0, :, ps]
            km = jnp.transpose(kmt_ref[0, ps, :])
            in_head = (lane_m >= hh * hd) & (lane_m < (hh + 1) * hd)
            kmh = jnp.where(in_head, km[0:2 * SUBLANES], 0.0).astype(BF16)
            sct = _dot_nt(kmh, q2)[0:nbp]
            rank = jnp.zeros((nbp, seq), F32)
            for bp in range(nb):
                rowv = sct[bp:bp + 1, :]
                beats = (rowv > sct) | ((rowv == sct) & (bp < bidx))
                rank = rank + jnp.where(beats & (bp < qblk), 1.0, 0.0)
            keep = (bidx < qblk) & (rank < MOBA_TOPK)
            pen_ref[2 * pair + hh] = jnp.where(keep, 0.0, NEG)
            own_lanes = (lane_q >= hh * hd) & (lane_q < (hh + 1) * hd)
            qh_ref[2 * pair + hh] = jnp.where(own_lanes, q2, jnp.zeros_like(q2))

        return [functools.partial(values_piece, 0), functools.partial(values_piece, 1),
                functools.partial(select_piece, 0), functools.partial(select_piece, 1)]

    def main(pair, fillers):
        ps = slice(pair * pw, (pair + 1) * pw)
        items = [(i, hh) for i in range(nb) for hh in range(2)]
        m_of, zero_of, o_of, s_of = {}, {}, {}, {}

        def base_of(i):
            return blk * (i * (i + 1) // 2)

        def logits_piece(k, j):
            i, hh = items[k]
            qs = slice(i * blk, (i + 1) * blk)
            sv = _dot_nt(k2_ref[0, j * blk:(j + 1) * blk, ps], qh_ref[2 * pair + hh, qs, :])
            s_of[k, j] = sv
            if j == i:
                mj = jnp.max(jnp.where(causal, sv, NEG), axis=0, keepdims=True)
            else:
                mj = jnp.max(sv, axis=0, keepdims=True) + pen_ref[2 * pair + hh, j:j + 1, qs]
            m_of[k] = mj if j == 0 else jnp.maximum(m_of[k], mj)

        def exp_piece(k, j):
            i, hh = items[k]
            qs = slice(i * blk, (i + 1) * blk)
            if j == 0:
                zero_of[k] = [jnp.zeros((1, blk), F32)] * _EXP_DEPTH
            shift = m_of[k] if j == i else m_of[k] - pen_ref[2 * pair + hh, j:j + 1, qs]
            sv = s_of.pop((k, j))
            for c in range(blk // _EXP_ROWS):
                r0 = base_of(i) + j * blk + c * _EXP_ROWS
                x = sv[c * _EXP_ROWS:(c + 1) * _EXP_ROWS]
                if j == i:
                    x = jnp.where(causal[c * _EXP_ROWS:(c + 1) * _EXP_ROWS], x, NEG)
                p = jnp.exp2(x - (shift + zero_of[k][c % _EXP_DEPTH]))
                p_ref[2 * pair + hh, r0:r0 + _EXP_ROWS, :] = p.astype(BF16)
                zero_of[k][c % _EXP_DEPTH] = jnp.minimum(p[_EXP_ROWS - 1:_EXP_ROWS], 0.0)

        def values(k):
            i, hh = items[k]
            ln = (i + 1) * blk
            ot = _dot(va_ref[2 * pair + hh, :, 0:ln], p_ref[2 * pair + hh, base_of(i):base_of(i) + ln, :])
            o_of[k] = ot[0:hd] / ot[hd:hd + 1]
            if hh == 1:
                both = jnp.concatenate([o_of.pop(k - 1), o_of.pop(k)], axis=0)
                o_ref[0, i * blk:(i + 1) * blk, ps] = jnp.transpose(both).astype(BF16)

        for j in range(items[0][0] + 1):
            logits_piece(0, j)
        for k in range(len(items)):
            ahead = [(k + 1, j) for j in range(items[k + 1][0] + 1)] if k + 1 < len(items) else []
            for j in range(items[k][0] + 1):
                exp_piece(k, j)
                if ahead:
                    logits_piece(*ahead.pop(0))
            for kj in ahead:
                logits_piece(*kj)
            values(k)
            if fillers and k >= 2 and k % 2 == 0:
                fillers.pop(0)()
        for f in fillers:
            f()

    for piece in setup_pieces(0):
        piece()
    for pair in range(n_pairs):
        main(pair, setup_pieces(pair + 1) if pair + 1 < n_pairs else [])


def _attn_prompt_call(q, k2, kmt, vt, *, n_heads, hd):
    n, s, d_att = q.shape
    nb = s // MOBA_BLOCK
    n_pairs = 2 if n_heads % 4 == 0 else 1
    wd = n_pairs * 2 * hd
    body = functools.partial(_attn_prompt_body, seq=s, nb=nb, hd=hd, n_pairs=n_pairs)
    tri = MOBA_BLOCK * (nb * (nb + 1) // 2)
    tok_spec = pl.BlockSpec((1, s, wd), lambda i, p: (i, 0, p))
    return pl.pallas_call(
        body, grid=(n, d_att // wd),
        in_specs=[tok_spec, tok_spec,
                  pl.BlockSpec((1, wd, LANES), lambda i, p: (i, p, 0)),
                  pl.BlockSpec((1, wd, s), lambda i, p: (i, p, 0))],
        out_specs=tok_spec,
        out_shape=jax.ShapeDtypeStruct((n, s, d_att), BF16),
        scratch_shapes=[pltpu.VMEM((2 * n_pairs, hd + 2 * SUBLANES, s), BF16),
                        pltpu.VMEM((2 * n_pairs, s, 2 * hd), BF16), pltpu.VMEM((2 * n_pairs, SUBLANES, s), F32),
                        pltpu.VMEM((2 * n_pairs, tri, MOBA_BLOCK), BF16)],
        compiler_params=pltpu.CompilerParams(dimension_semantics=("arbitrary", "arbitrary"),
                                             vmem_limit_bytes=VMEM_LIMIT),
        name="moba_prompt",
    )(q, k2, kmt, vt)


def _post_body(x_ref, o_ref, ag_ref, gb_ref, mod_ref, wba_ref, wout_ref, gffn_ref, wup_ref, wconv_ref, bconv_ref,
               wdown_ref, cst_ref, y_ref, conv_ref, gbuf, act_ref, *, tm, rows_per_tok, halo, nt, pc, down_groups, d_ff):
    t = pl.program_id(1)
    x = x_ref[0]
    gt_m = mod_ref[0, 2]
    sh_f = mod_ref[0, 3]
    sc_f = mod_ref[0, 4]
    gt_f = mod_ref[0, 5]
    oa = _dot(o_ref[0], wba_ref[...])
    merged = ag_ref[0].astype(F32) + gb_ref[0].astype(F32) * oa
    x1 = x + gt_m * _dot(merged.astype(BF16), wout_ref[...])
    ms = jnp.mean(x1 * x1, axis=-1, keepdims=True)
    h2 = x1 * lax.rsqrt(ms + EPS) * gffn_ref[...]
    h2 = (h2 * (1.0 + sc_f) + sh_f).astype(BF16)

    @pl.when(t == 0)
    def _():
        gbuf[0:halo, :] = cst_ref[0]

    @pl.when(t > 0)
    def _():
        gbuf[0:halo, :] = gbuf[tm:tm + halo, :]

    r = rows_per_tok
    n_pc = d_ff // pc

    def up(c):
        cs = slice(c * pc, (c + 1) * pc)
        fg = _dot(h2, wup_ref[:, cs])
        fv = _dot(h2, wup_ref[:, d_ff + c * pc:d_ff + (c + 1) * pc])
        gbuf[halo:halo + tm, cs] = fg
        return fg, fv

    def activate(c, fg, fv):
        cs = slice(c * pc, (c + 1) * pc)
        conv = bconv_ref[:, cs]
        conv = conv + gbuf[halo - 2 * r:halo - 2 * r + tm, cs] * wconv_ref[0:1, cs]
        conv = conv + gbuf[halo - r:halo - r + tm, cs] * wconv_ref[1:2, cs]
        conv = conv + fg * wconv_ref[2:3, cs]
        act_ref[:, cs] = (conv * _sigmoid(conv) * fv).astype(BF16)

    bounds = [round(g * n_pc / down_groups) * pc for g in range(down_groups + 1)]
    acc = None
    nxt = up(0)
    for c in range(n_pc):
        cur = nxt
        if c + 1 < n_pc:
            nxt = up(c + 1)
        activate(c, *cur)
        if (c + 1) * pc in bounds[1:]:
            lo = bounds[bounds.index((c + 1) * pc) - 1]
            part = _dot(act_ref[:, lo:(c + 1) * pc], wdown_ref[lo:(c + 1) * pc, :])
            acc = part if acc is None else acc + part
    y_ref[0] = x1 + gt_f * acc

    @pl.when(t == nt - 1)
    def _():
        keep = (CONV_W - 1) * r
        conv_ref[0] = gbuf[halo + tm - keep:halo + tm, :]


def _post_call(x, o, ag, gb, mod, cstate, wts, *, tm, rows_per_tok, halo, pc, down_groups):
    n, s, d = x.shape
    nt = s // tm
    d_att = o.shape[-1]
    d_ff = wts["w_down"].shape[0]
    keep = (CONV_W - 1) * rows_per_tok
    rm = mod.shape[2]
    body = functools.partial(_post_body, tm=tm, rows_per_tok=rows_per_tok, halo=halo, nt=nt, pc=pc,
                             down_groups=down_groups, d_ff=d_ff)
    row_spec = lambda w: pl.BlockSpec((1, tm, w), lambda i, t: (i, t, 0))
    in_specs = [
        row_spec(d), row_spec(d_att), row_spec(d), row_spec(d),
        pl.BlockSpec((1, N_MOD, rm, d), lambda i, t: (i, 0, 0, 0)),
        _const_spec(wts["w_ba"].shape), _const_spec(wts["w_out"].shape), _const_spec((1, d)),
        _const_spec(wts["w_up"].shape), _const_spec((CONV_W, d_ff)), _const_spec((1, d_ff)),
        _const_spec(wts["w_down"].shape),
        pl.BlockSpec((1, halo, d_ff), lambda i, t: (i, 0, 0)),
    ]
    out_specs = [row_spec(d), pl.BlockSpec((1, keep, d_ff), lambda i, t: (i, 0, 0))]
    out_shape = [jax.ShapeDtypeStruct((n, s, d), F32), jax.ShapeDtypeStruct((n, keep, d_ff), F32)]
    return pl.pallas_call(
        body, grid=(n, nt), in_specs=in_specs, out_specs=out_specs, out_shape=out_shape,
        scratch_shapes=[pltpu.VMEM((halo + tm, d_ff), F32), pltpu.VMEM((tm, d_ff), BF16)],
        compiler_params=pltpu.CompilerParams(dimension_semantics=("arbitrary", "arbitrary"),
                                             vmem_limit_bytes=VMEM_LIMIT),
        name="merge_ffn",
    )(x, o, ag, gb, mod, wts["w_ba"], wts["w_out"], wts["g_ffn"], wts["w_up"], wts["w_conv"], wts["b_conv"],
      wts["w_down"], cstate)


def _kmean_body(pt_ref, ck_ref, km_ref, buf, sem, *, n_seq, n_pages, ppb):
    n = pl.program_id(0)
    rows = km_ref.shape[1]
    bpg = _KM_GROUP // ppb
    n_groups = n_pages // _KM_GROUP

    def copy(seq, p, slot):
        return pltpu.make_async_copy(ck_ref.at[pt_ref[seq, p]], buf.at[slot], sem.at[slot])

    def start_group(seq, g, half):
        for k in range(_KM_GROUP):
            copy(seq, g * _KM_GROUP + k, half * _KM_GROUP + k).start()

    @pl.when(n == 0)
    def _():
        start_group(0, 0, 0)
        start_group(0, 1, 1)

    km_ref[0] = jnp.zeros((rows, LANES), F32)
    lane = lax.broadcasted_iota(jnp.int32, (_KM_ROWS, LANES), 1)

    def pair(g2, carry):
        for half in range(2):
            g = g2 * 2 + half
            for k in range(_KM_GROUP):
                copy(n, g * _KM_GROUP + k, half * _KM_GROUP + k).wait()
            for rc in range(rows // _KM_ROWS):
                rs = slice(rc * _KM_ROWS, (rc + 1) * _KM_ROWS)
                upd = jnp.zeros((_KM_ROWS, LANES), F32)
                for bb in range(bpg):
                    slot = half * _KM_GROUP + bb * ppb
                    x = buf[slot, rs, :]
                    for pp in range(1, ppb):
                        x = x + buf[slot + pp, rs, :]
                    col = jnp.sum(x, axis=-1, keepdims=True)
                    upd = jnp.where(lane == g * bpg + bb, col, upd)
                km_ref[0, rs, :] = km_ref[0, rs, :] + upd * (1.0 / MOBA_BLOCK)
            nxt = g + 2

            @pl.when(nxt < n_groups)
            def _():
                start_group(n, nxt, half)

            @pl.when((nxt >= n_groups) & (n + 1 < n_seq))
            def _():
                start_group(n + 1, nxt - n_groups, half)
        return carry

    lax.fori_loop(0, n_groups // 2, pair, 0)


def _kmean_call(page_table, ck_t):
    n, n_pages = page_table.shape
    _, rows, page = ck_t.shape
    ppb = MOBA_BLOCK // page
    assert n_pages % (2 * _KM_GROUP) == 0 and rows % _KM_ROWS == 0
    body = functools.partial(_kmean_body, n_seq=n, n_pages=n_pages, ppb=ppb)
    grid_spec = pltpu.PrefetchScalarGridSpec(
        num_scalar_prefetch=1, grid=(n,),
        in_specs=[pl.BlockSpec(memory_space=pl.ANY)],
        out_specs=pl.BlockSpec((1, rows, LANES), lambda i, pt: (i, 0, 0)),
        scratch_shapes=[pltpu.VMEM((2 * _KM_GROUP, rows, page), F32), pltpu.SemaphoreType.DMA((2 * _KM_GROUP,))],
    )
    return pl.pallas_call(
        body, grid_spec=grid_spec,
        out_shape=jax.ShapeDtypeStruct((n, rows, LANES), F32),
        compiler_params=pltpu.CompilerParams(dimension_semantics=("arbitrary",), vmem_limit_bytes=VMEM_LIMIT),
        name="cache_block_means",
    )(page_table, ck_t)


def _topk_body(q_ref, km_ref, sel_ref, *, n_heads, hd, n_blk, top, unroll):
    n_seq = q_ref.shape[0]
    rows = n_heads * SUBLANES
    lane_q = lax.broadcasted_iota(jnp.int32, (SUBLANES, n_heads * hd), 1)
    lane = lax.broadcasted_iota(jnp.int32, (unroll * rows, LANES), 1).astype(F32)

    def per_group(ig, carry):
        scs = []
        for k in range(unroll):
            i = ig * unroll + k
            q8 = q_ref[i]
            qbd = jnp.concatenate(
                [jnp.where((lane_q >= hh * hd) & (lane_q < (hh + 1) * hd), q8, 0.0) for hh in range(n_heads)], axis=0)
            scs.append(_dot(qbd.astype(BF16), km_ref[i].astype(BF16)))
        sc = jnp.where(lane < n_blk, jnp.concatenate(scs, axis=0), -jnp.inf)
        out = jnp.zeros((unroll * rows, LANES), F32)
        for r in range(top):
            m = jnp.max(sc, axis=-1, keepdims=True)
            idx = jnp.min(jnp.where(sc == m, lane, float(LANES)), axis=-1, keepdims=True)
            out = jnp.where(lane == r, idx, out)
            sc = jnp.where(lane == idx, -jnp.inf, sc)
        out = out.astype(jnp.int32)
        for k in range(unroll):
            sel_ref[ig * unroll + k] = out[k * rows:(k + 1) * rows]
        return carry

    lax.fori_loop(0, n_seq // unroll, per_group, 0)


def _topk_call(q8, kmt, *, n_heads, hd, n_blk, top):
    n = q8.shape[0]
    rows = n_heads * SUBLANES
    unroll = max(u for u in (8, 4, 2, 1) if n % u == 0)
    body = functools.partial(_topk_body, n_heads=n_heads, hd=hd, n_blk=n_blk, top=top, unroll=unroll)
    return pl.pallas_call(
        body, grid=(1,),
        in_specs=[pl.BlockSpec(q8.shape, lambda i: (0, 0, 0)), pl.BlockSpec(kmt.shape, lambda i: (0, 0, 0))],
        out_specs=pl.BlockSpec((n, rows, LANES), lambda i: (0, 0, 0)),
        out_shape=jax.ShapeDtypeStruct((n, rows, LANES), jnp.int32),
        compiler_params=pltpu.CompilerParams(vmem_limit_bytes=VMEM_LIMIT),
        name="block_topk",
    )(q8, kmt)


def _attn_sample_body(pt_ref, sel_ref, qt_ref, knt_ref, vnt_ref, lseq_ref, ltok_ref, ck_ref, cv_ref, o_ref, kbuf, vbuf,
                      lg_ref, p_ref, sem,
                      *, n_seq, n_heads, hd, n_tok, top, ppb):
    n = pl.program_id(0)
    n_sel = top * ppb
    n_grp = n_heads * n_tok
    slot = lax.rem(n, 2)

    def gather(seq, sl):
        for hh in range(n_heads):
            for s in range(n_tok):
                g = hh * n_tok + s
                for j in range(n_sel):
                    r, pp = divmod(j, ppb)
                    page = pt_ref[seq, sel_ref[seq, g * top + r] * ppb + pp]
                    pltpu.make_async_copy(ck_ref.at[page, hh], kbuf.at[sl, g, j], sem.at[sl]).start()
                    pltpu.make_async_copy(cv_ref.at[page, hh], vbuf.at[sl, g, j], sem.at[sl]).start()

    @pl.when(n == 0)
    def _():
        gather(0, 0)

    @pl.when(n + 1 < n_seq)
    def _():
        gather(n + 1, 1 - slot)

    for g in range(n_grp):
        for j in range(n_sel):
            pltpu.make_async_copy(ck_ref.at[0, 0], kbuf.at[slot, g, j], sem.at[slot]).wait()
            pltpu.make_async_copy(cv_ref.at[0, 0], vbuf.at[slot, g, j], sem.at[slot]).wait()

    mine = lseq_ref[...] == n
    lane_tok = ltok_ref[...]
    pad_rows = [jnp.full((1, LANES), NEG, F32)] * (SUBLANES - n_sel - 1)
    for hh in range(n_heads):
        rs = slice(hh * hd, (hh + 1) * hd)
        knt = knt_ref[rs, :]
        for s in range(n_tok):
            g = hh * n_tok + s
            qb = jnp.broadcast_to(qt_ref[0, rs, s:s + 1], (hd, LANES))
            rows = [jnp.sum(kbuf[slot, g, j] * qb, axis=0, keepdims=True) for j in range(n_sel)]
            own = jnp.sum(knt * qb, axis=0, keepdims=True)
            rows.append(jnp.where(mine & (lane_tok <= s), own, NEG))
            lg_ref[g] = jnp.concatenate(rows + pad_rows, axis=0)

    lg = lg_ref[...]
    m = jnp.max(jnp.max(lg, axis=1), axis=-1, keepdims=True)
    mb = jnp.broadcast_to(m, (n_grp, LANES))
    for g in range(n_grp):
        p_ref[g] = jnp.exp2(lg_ref[g] - mb[g:g + 1, :])
    l = jnp.sum(jnp.sum(p_ref[...], axis=1), axis=-1, keepdims=True)
    inv = jnp.broadcast_to(1.0 / l, (n_grp, LANES))

    lane_o = lax.broadcasted_iota(jnp.int32, (hd, LANES), 1)
    outs = []
    for hh in range(n_heads):
        rs = slice(hh * hd, (hh + 1) * hd)
        vnt = vnt_ref[rs, :]
        out_h = jnp.zeros((hd, LANES), F32)
        for s in range(n_tok):
            g = hh * n_tok + s
            pg = p_ref[g] * inv[g:g + 1, :]
            acc = vnt * pg[n_sel:n_sel + 1, :]
            for j in range(n_sel):
                acc = acc + vbuf[slot, g, j] * pg[j:j + 1, :]
            col = jnp.sum(acc, axis=-1, keepdims=True)
            out_h = jnp.where(lane_o == s, col, out_h)
        outs.append(out_h)
    o_ref[0] = jnp.transpose(jnp.concatenate(outs, axis=0))[0:SUBLANES]


def _attn_sample_call(page_table, sel, qt, knt, vnt, lane_seq, lane_tok, ck_t, cv_t, *, n_heads, hd, n_tok, top):
    n = page_table.shape[0]
    page = ck_t.shape[-1]
    ppb = MOBA_BLOCK // page
    d_att = n_heads * hd
    n_grp = n_heads * n_tok
    assert top * ppb + 1 <= SUBLANES
    body = functools.partial(_attn_sample_body, n_seq=n, n_heads=n_heads, hd=hd, n_tok=n_tok, top=top, ppb=ppb)
    blk = pl.BlockSpec((1, d_att, LANES), lambda i, pt, sl: (i, 0, 0))
    whole = lambda shape: pl.BlockSpec(shape, lambda i, pt, sl: (0,) * len(shape))
    grid_spec = pltpu.PrefetchScalarGridSpec(
        num_scalar_prefetch=2, grid=(n,),
        in_specs=[blk, whole((d_att, LANES)), whole((d_att, LANES)), whole((1, LANES)), whole((1, LANES)),
                  pl.BlockSpec(memory_space=pl.ANY), pl.BlockSpec(memory_space=pl.ANY)],
        out_specs=pl.BlockSpec((1, SUBLANES, d_att), lambda i, pt, sl: (i, 0, 0)),
        scratch_shapes=[pltpu.VMEM((2, n_grp, top * ppb, hd, page), F32),
                        pltpu.VMEM((2, n_grp, top * ppb, hd, page), F32),
                        pltpu.VMEM((n_grp, SUBLANES, LANES), F32),
                        pltpu.VMEM((n_grp, SUBLANES, LANES), F32),
                        pltpu.SemaphoreType.DMA((2,))],
    )
    return pl.pallas_call(
        body, grid_spec=grid_spec,
        out_shape=jax.ShapeDtypeStruct((n, SUBLANES, d_att), F32),
        compiler_params=pltpu.CompilerParams(dimension_semantics=("arbitrary",), vmem_limit_bytes=VMEM_LIMIT),
        name="moba_sample",
    )(page_table, sel, qt, knt, vnt, lane_seq, lane_tok, ck_t, cv_t)


def _layer_weights(l, g_norm_mix, w_in, g_q, g_k, w_pool_group, pool_scale, w_branch_pool, w_branch_attn, w_out,
                   g_norm_ffn, w_up, w_conv, b_conv, w_down, n_heads, hd):
    d = w_in.shape[1]
    pw = w_pool_group.shape[1] * w_pool_group.shape[2]
    d_att = n_heads * hd
    wi = w_in[l].astype(BF16)
    blk = jnp.arange(MXU_DIM) // hd
    return {
        "g_mix": g_norm_mix[l].reshape(1, d),
        "w_in": wi,
        "g_q": g_q[l].reshape(1, d_att), "g_k": g_k[l].reshape(d_att, 1),
        "bd": jnp.where(blk[:, None] == blk[None, :], 1.0 / hd, 0.0).astype(BF16),
        "w_pg": w_pool_group[l].astype(BF16), "p_scale": pool_scale[l].reshape(1, pw),
        "w_bp": w_branch_pool[l].astype(BF16),
        "w_ba": w_branch_attn[l].astype(BF16), "w_out": w_out[l].astype(BF16),
        "g_ffn": g_norm_ffn[l].reshape(1, d), "w_up": w_up[l].astype(BF16),
        "w_conv": w_conv[l], "b_conv": b_conv[l].reshape(1, -1), "w_down": w_down[l].astype(BF16),
    }


def kernel(x_prompt, x_sample, cache_k, cache_v, state_pool, state_ffn_conv, page_table, c_prompt, c_sample, w_ada, b_ada, g_norm_mix, w_in, g_q, g_k, w_pool_group, pool_scale, w_branch_pool, w_branch_attn, w_out, g_norm_ffn, w_up, w_conv, b_conv, w_down):
    nb_p, seq, d = x_prompt.shape
    nb_s, n_tok, _ = x_sample.shape
    depth, n_phys, page, n_heads, hd = cache_k.shape
    d_att = n_heads * hd
    pw = state_pool.shape[-1]
    d_ff = w_down.shape[1]
    n_pages = page_table.shape[1]
    past = n_pages * page
    assert MOBA_BLOCK % page == 0 and past % MOBA_BLOCK == 0 and seq % MOBA_BLOCK == 0
    assert n_tok <= SUBLANES and nb_s * n_tok <= LANES
    n_blk = past // MOBA_BLOCK
    top = min(MOBA_TOPK, n_blk)
    assert 0 < n_blk <= LANES and seq // MOBA_BLOCK <= SUBLANES
    tm = 256
    tm_ffn = 512 if seq % 512 == 0 else tm
    pc, dg = MXU_DIM, 3
    assert d_ff % pc == 0
    rows_s = nb_s * n_tok

    y_p, y_s = x_prompt, x_sample.transpose(1, 0, 2).reshape(1, rows_s, d)
    halo_p = 4 * SUBLANES
    zero_pool = jnp.zeros((nb_p, halo_p, pw), F32)
    zero_conv = jnp.zeros((nb_p, SUBLANES, d_ff), F32)
    outs = [[] for _ in range(8)]
    for l in range(depth):
        wts = _layer_weights(l, g_norm_mix, w_in, g_q, g_k, w_pool_group, pool_scale, w_branch_pool, w_branch_attn,
                             w_out, g_norm_ffn, w_up, w_conv, b_conv, w_down, n_heads, hd)
        mod = _ada_call(jnp.concatenate([c_prompt, c_sample], axis=0), w_ada[l].astype(BF16), b_ada[l])
        mod_p = mod[:nb_p].reshape(nb_p, N_MOD, 1, d)
        mod_s = jnp.tile(mod[nb_p:].reshape(nb_s, N_MOD, d), (n_tok, 1, 1)).transpose(1, 0, 2)[None]

        ck_t = cache_k[l].transpose(0, 2, 3, 1)
        cv_t = cache_v[l].transpose(0, 2, 3, 1)
        ck_flat = ck_t.reshape(n_phys, d_att, page)
        gpsq, steps = n_pages // _KM_GROUP, nb_p * (seq // tm)
        gps = nb_s * gpsq // steps
        fuse = n_pages % _KM_GROUP == 0 and gps >= 1 and gps * steps == nb_s * gpsq and gpsq % gps == 0
        res = _inproj_call(y_p, mod_p, zero_pool, wts, tm=tm, rows_per_tok=1, halo=halo_p, pos0=0,
                           n_heads=n_heads, hd=hd, block_means=True, pages=(page_table, ck_flat) if fuse else None)
        kt, vt, q, k2, kmt, ag, gb, pool_p = res[:8]
        kmt_s = res[8] if fuse else _kmean_call(page_table, ck_flat)
        o = _attn_prompt_call(q, k2, kmt, vt, n_heads=n_heads, hd=hd)
        y_p, conv_p = _post_call(y_p, o, ag, gb, mod_p, zero_conv, wts, tm=tm_ffn, rows_per_tok=1, halo=SUBLANES, pc=pc,
                                 down_groups=dg)
        to_rows = lambda a: a.reshape(nb_p, n_heads, hd, seq).transpose(0, 3, 1, 2)
        outs[0].append(to_rows(kt)); outs[1].append(to_rows(vt)); outs[2].append(pool_p); outs[3].append(conv_p)

        pool_state = state_pool[l].transpose(1, 0, 2).reshape(1, POOL_CTX * nb_s, pw)
        conv_state = state_ffn_conv[l].transpose(1, 0, 2).reshape(1, (CONV_W - 1) * nb_s, d_ff)
        kt_s, vt_s, q_s, _, _, ag_s, gb_s, pool_s = _inproj_call(
            y_s, mod_s, pool_state, wts, tm=rows_s, rows_per_tok=nb_s, halo=POOL_CTX * nb_s, pos0=past,
            n_heads=n_heads, hd=hd, block_means=False)
        q_ns = q_s.reshape(n_tok, nb_s, d_att).transpose(1, 0, 2).astype(F32)
        q8 = jnp.pad(q_ns, ((0, 0), (0, SUBLANES - n_tok), (0, 0)))
        sel = _topk_call(q8, kmt_s, n_heads=n_heads, hd=hd, n_blk=n_blk, top=top)
        sel = sel.reshape(nb_s, n_heads, SUBLANES, LANES)[:, :, :n_tok, :top].reshape(nb_s, n_heads * n_tok * top)
        qt = jnp.pad(q_ns.transpose(0, 2, 1), ((0, 0), (0, 0), (0, LANES - n_tok)))
        lane_pad = lambda a: jnp.pad(a, ((0, 0), (0, LANES - rows_s)))
        lanes = jnp.arange(LANES)
        lane_seq = jnp.where(lanes < rows_s, lanes % nb_s, -1).astype(jnp.int32).reshape(1, LANES)
        lane_tok = (lanes // nb_s).astype(jnp.int32).reshape(1, LANES)
        ot = _attn_sample_call(page_table, sel, qt, lane_pad(kt_s[0]), lane_pad(vt_s[0]), lane_seq, lane_tok, ck_t, cv_t,
                               n_heads=n_heads, hd=hd, n_tok=n_tok, top=top)
        o_s = ot[:, :n_tok].transpose(1, 0, 2).reshape(1, rows_s, d_att).astype(BF16)
        y_s, conv_s = _post_call(y_s, o_s, ag_s, gb_s, mod_s, conv_state, wts, tm=rows_s, rows_per_tok=nb_s,
                                 halo=(CONV_W - 1) * nb_s, pc=pc, down_groups=dg)
        to_rows_s = lambda a: a[0].reshape(n_heads, hd, n_tok, nb_s).transpose(3, 2, 0, 1)
        outs[4].append(to_rows_s(kt_s)); outs[5].append(to_rows_s(vt_s))
        outs[6].append(pool_s.reshape(POOL_CTX, nb_s, pw).transpose(1, 0, 2))
        outs[7].append(conv_s.reshape(CONV_W - 1, nb_s, d_ff).transpose(1, 0, 2))

    y_s_out = y_s.reshape(n_tok, nb_s, d).transpose(1, 0, 2)
    st = [jnp.stack(v) for v in outs]
    return (y_p, y_s_out, st[0], st[1], st[2], st[3], st[4], st[5], st[6], st[7])
```

```python
import functools

import jax
import jax.numpy as jnp
from jax import lax
from jax.experimental import pallas as pl
from jax.experimental.pallas import tpu as pltpu

F32 = jnp.float32
BF16 = jnp.bfloat16

MOBA_BLOCK = 256
MOBA_TOPK = 3
POOL_WINDOWS = (2, 4, 8, 16)
POOL_CTX = max(POOL_WINDOWS) - 1
CONV_W = 3
N_MOD = 6
EPS = 1e-6
NEG = -1e30
LOG2E = 1.4426950408889634

LANES = 128
SUBLANES = 8
MXU_DIM = 256
VMEM_LIMIT = 60 * 1024 * 1024
_EXP_ROWS = 64
_EXP_DEPTH = 1
_KM_GROUP = 16
_KM_ROWS = 64


def _const_spec(shape):
    nd = len(shape)
    return pl.BlockSpec(shape, lambda *_: (0,) * nd, pipeline_mode=pl.Buffered(1))


def _dot(a, b):
    return jnp.dot(a, b, preferred_element_type=F32)


def _dot_nt(a, b):
    return lax.dot_general(a, b, (((1,), (1,)), ((), ())), preferred_element_type=F32)


def _sigmoid(x):
    return 1.0 / (1.0 + jnp.exp(-x))


def _ada_body(c_ref, w_ref, b_ref, o_ref):
    c = c_ref[...]
    a = (c * _sigmoid(c)).astype(BF16)
    o_ref[...] = _dot(a, w_ref[...]) + b_ref[...]


def _ada_call(c, w_ada, b_ada):
    n, d = c.shape
    dm = w_ada.shape[1]
    bn = d
    return pl.pallas_call(
        _ada_body,
        grid=(dm // bn,),
        in_specs=[pl.BlockSpec((n, d), lambda j: (0, 0)),
                  pl.BlockSpec((d, bn), lambda j: (0, j)),
                  pl.BlockSpec((1, bn), lambda j: (0, j))],
        out_specs=pl.BlockSpec((n, bn), lambda j: (0, j)),
        out_shape=jax.ShapeDtypeStruct((n, dm), F32),
        name="ada_mod",
    )(c, w_ada, b_ada.reshape(1, dm))


def _inproj_body(*refs, tm, rows_per_tok, halo, pos0, nt, n_heads, hd, scale, block_means, stream):
    if stream is not None:
        pt_ref, refs = refs[0], refs[1:]
    (x_ref, mod_ref, gmix_ref, win_ref, gq_ref, gk_ref, bd_ref, wpg_ref, psc_ref,
     wbp_ref, st_ref) = refs[:11]
    refs = refs[11:]
    if stream is not None:
        ck_ref, refs = refs[0], refs[1:]
    kt_ref, vt_ref, q_ref, k2_ref, kmt_ref, ag_ref, gb_ref, pool_ref = refs[:8]
    refs = refs[8:]
    if stream is not None:
        kms_ref, ubuf, wbuf, wkv_t, pbuf, psem = refs
    else:
        ubuf, wbuf, wkv_t = refs
    n_lvl = max(len(POOL_WINDOWS) - 1, 1)
    t = pl.program_id(1)
    d_att = n_heads * hd
    d_model = x_ref.shape[-1]
    cw = MXU_DIM

    todo = []
    if stream is not None:
        gps, gpsq, total, ppb = stream
        bpg = _KM_GROUP // ppb
        step = pl.program_id(0) * nt + t
        n_steps = total // gps
        km_rows = kms_ref.shape[1]
        lane_s = lax.broadcasted_iota(jnp.int32, (_KM_ROWS, LANES), 1)
        side_set = lax.rem(step, 2)

        def batch_copies(b, sset):
            out = []
            for k in range(gps):
                gidx = b * gps + k
                seq, g = lax.div(gidx, gpsq), lax.rem(gidx, gpsq)
                for j in range(_KM_GROUP):
                    slot = (sset * gps + k) * _KM_GROUP + j
                    out.append(pltpu.make_async_copy(ck_ref.at[pt_ref[seq, g * _KM_GROUP + j]], pbuf.at[slot],
                                                     psem.at[sset]))
            return out

        @pl.when(step == 0)
        def _():
            for b0 in range(min(2, n_steps)):
                for cp in batch_copies(jnp.int32(b0), b0):
                    cp.start()

        for slot in range(gps * _KM_GROUP):
            pltpu.make_async_copy(ck_ref.at[0], pbuf.at[side_set * gps * _KM_GROUP + slot], psem.at[side_set]).wait()

        @pl.when(lax.rem(step * gps, gpsq) == 0)
        def _():
            kms_ref[0] = jnp.zeros((km_rows, LANES), F32)

        def stream_reduce(k, rc):
            g = lax.rem(step * gps + k, gpsq)
            rs = slice(rc * _KM_ROWS, (rc + 1) * _KM_ROWS)
            upd = jnp.zeros((_KM_ROWS, LANES), F32)
            for bb in range(bpg):
                slot = (side_set * gps + k) * _KM_GROUP + bb * ppb
                xs = pbuf[slot, rs, :]
                for pp in range(1, ppb):
                    xs = xs + pbuf[slot + pp, rs, :]
                upd = jnp.where(lane_s == g * bpg + bb, jnp.sum(xs, axis=-1, keepdims=True), upd)
            kms_ref[0, rs, :] = kms_ref[0, rs, :] + upd * (1.0 / MOBA_BLOCK)

        todo = [(k, rc) for k in range(gps) for rc in range(km_rows // _KM_ROWS)]

    def side(n_pieces=1):
        for _ in range(n_pieces):
            if todo:
                stream_reduce(*todo.pop(0))

    pw = wpg_ref.shape[0] * wpg_ref.shape[2]
    q0, k0, v0, g0 = pw, pw + d_att, pw + 2 * d_att, pw + 3 * d_att

    @pl.when((pl.program_id(0) == 0) & (t == 0))
    def _():
        wkv_t[0] = jnp.transpose(win_ref[:, k0:v0].astype(F32)).astype(BF16)
        wkv_t[1] = jnp.transpose(win_ref[:, v0:g0].astype(F32)).astype(BF16)

    x = x_ref[0]
    sh_m = mod_ref[0, 0]
    sc_m = mod_ref[0, 1]
    ms = jnp.mean(x * x, axis=-1, keepdims=True)
    h = x * lax.rsqrt(ms + EPS) * gmix_ref[...]
    h = (h * (1.0 + sc_m) + sh_m).astype(BF16)

    u = jnp.concatenate([_dot(h, win_ref[:, c * cw:(c + 1) * cw]) for c in range(pw // cw)], axis=1)

    @pl.when(t == 0)
    def _():
        ubuf[0:halo, :] = st_ref[0]

    @pl.when(t > 0)
    def _():
        ubuf[0:halo, :] = ubuf[tm:tm + halo, :]

    ubuf[halo:halo + tm, :] = u
    gw = LANES

    def window(g):
        w = POOL_WINDOWS[g]
        cs = slice(g * gw, (g + 1) * gw)
        n_rows = halo + tm
        src, valid_from, lvl, d = None, 0, 0, 1
        while 2 * d < w:
            s = d * rows_per_tok
            a = -(-s // SUBLANES) * SUBLANES
            dst = wbuf.at[g * n_lvl + lvl]
            if src is None:
                dst[a:n_rows, :] = ubuf[a:n_rows, cs] + ubuf[a - s:n_rows - s, cs]
            else:
                dst[a:n_rows, :] = src[a:n_rows, :] + src[a - s:n_rows - s, :]
            valid_from = max(a, valid_from + s)
            src, lvl, d = dst, lvl + 1, 2 * d
        s = d * rows_per_tok
        assert halo - s >= valid_from
        if src is None:
            acc = ubuf[halo:n_rows, cs] + ubuf[halo - s:n_rows - s, cs]
        else:
            acc = src[halo:n_rows, :] + src[halo - s:n_rows - s, :]
        if pos0 >= w - 1:
            cnt = float(w)
        else:
            assert rows_per_tok == 1
            row = lax.broadcasted_iota(jnp.int32, (tm, 1), 0) + (t * tm + pos0 + 1)
            cnt = jnp.minimum(float(w), row.astype(F32))
        return (acc / cnt - u[:, cs]).astype(BF16)

    def key_heads(kt_piece, c):
        outs = []
        for hh in range(cw // hd):
            rs = slice(c * cw + hh * hd, c * cw + (hh + 1) * hd)
            kh = kt_piece[hh * hd:(hh + 1) * hd]
            msk = jnp.sum(kh * kh, axis=0, keepdims=True) * (1.0 / hd)
            outs.append(kh * lax.rsqrt(msk + EPS) * gk_ref[rs])
            kt_ref[0, rs, :] = outs[-1]
        return jnp.concatenate(outs, axis=0)

    n_kc = d_att // cw
    pooled, kn = [], []
    for c in range(n_kc):
        ktc = _dot_nt(wkv_t[0, c * cw:(c + 1) * cw, :], h)
        pooled.append(window(c))
        side()
        kn.append((ktc, c))
    for c in range(n_kc):
        vt_ref[0, c * cw:(c + 1) * cw, :] = _dot_nt(wkv_t[1, c * cw:(c + 1) * cw, :], h)
        kn[c] = key_heads(*kn[c])
        side()
    kn = jnp.concatenate(kn, axis=0)
    qs = []
    for c in range(n_kc):
        qs.append(_dot(h, win_ref[:, q0 + c * cw:q0 + (c + 1) * cw]))
        if n_kc + c < len(POOL_WINDOWS):
            pooled.append(window(n_kc + c))
        side()
    for g in range(len(pooled), len(POOL_WINDOWS)):
        pooled.append(window(g))
    k2_ref[0] = jnp.transpose(kn).astype(BF16)

    @pl.when(t == 0)
    def _():
        kmt_ref[0] = jnp.zeros((d_att, LANES), F32)

    if block_means:
        lane_b = lax.broadcasted_iota(jnp.int32, (d_att, LANES), 1)
        for bb in range(tm // MOBA_BLOCK):
            part = kn[:, bb * MOBA_BLOCK:bb * MOBA_BLOCK + LANES]
            for c in range(1, MOBA_BLOCK // LANES):
                part = part + kn[:, bb * MOBA_BLOCK + c * LANES:bb * MOBA_BLOCK + (c + 1) * LANES]
            col = jnp.sum(part, axis=-1, keepdims=True) * (1.0 / MOBA_BLOCK)
            kmt_ref[0] = jnp.where(lane_b == t * (tm // MOBA_BLOCK) + bb, col, kmt_ref[0])

    ys = [_dot(pooled[g], wpg_ref[g]) for g in range(len(POOL_WINDOWS))]
    split = []
    for c in range(n_kc):
        qq = qs[c] * qs[c]
        hi = qq.astype(BF16)
        split.append((hi, (qq - hi.astype(F32)).astype(BF16)))
    side()
    y = (jnp.concatenate(ys, axis=1) * psc_ref[...]).astype(BF16)
    a = []
    for c in range(d_model // cw):
        a.append(_dot(y, wbp_ref[:, c * cw:(c + 1) * cw]))
        if c < n_kc:
            msq = _dot(split[c][0], bd_ref[...]) + _dot(split[c][1], bd_ref[...])
            cs = slice(c * cw, (c + 1) * cw)
            q_ref[0, :, cs] = (qs[c] * lax.rsqrt(msq + EPS) * gq_ref[:, cs] * scale).astype(BF16)
        side()

    n_gc = 2 * d_model // cw
    gate = [None] * n_gc

    def finish_gate(c):
        sg = _sigmoid(gate[c])
        lo_c = c * cw
        if lo_c < d_model:
            ag_ref[0, :, lo_c:lo_c + cw] = (sg * a[c]).astype(BF16)
        else:
            gb_ref[0, :, lo_c - d_model:lo_c - d_model + cw] = sg.astype(BF16)

    for c in range(n_gc):
        gate[c] = _dot(h, win_ref[:, g0 + c * cw:g0 + (c + 1) * cw])
        if c > 0:
            finish_gate(c - 1)
        side()
    finish_gate(n_gc - 1)
    side(len(todo))

    @pl.when(t == nt - 1)
    def _():
        keep = POOL_CTX * rows_per_tok
        pool_ref[0] = ubuf[halo + tm - keep:halo + tm, :]

    if stream is not None:
        @pl.when(step + 2 < n_steps)
        def _():
            for cp in batch_copies(step + 2, side_set):
                cp.start()


def _inproj_call(x, mod, state, wts, *, tm, rows_per_tok, halo, pos0, n_heads, hd, block_means, pages=None):
    n, s, d = x.shape
    nt = s // tm
    d_att = n_heads * hd
    pw = wts["p_scale"].shape[1]
    keep = POOL_CTX * rows_per_tok
    stream = None
    if pages is not None:
        page_table, ck_t = pages
        n_seq, n_pages = page_table.shape
        _, km_rows, page = ck_t.shape
        gpsq = n_pages // _KM_GROUP
        total = n_seq * gpsq
        gps = total // (n * nt)
        assert n_pages % _KM_GROUP == 0 and total == gps * n * nt and gps >= 1 and km_rows % _KM_ROWS == 0
        assert gpsq % gps == 0
        stream = (gps, gpsq, total, MOBA_BLOCK // page)
    body = functools.partial(_inproj_body, tm=tm, rows_per_tok=rows_per_tok, halo=halo, pos0=pos0, nt=nt,
                             n_heads=n_heads, hd=hd, scale=hd ** -0.5 * LOG2E, block_means=block_means, stream=stream)
    rm = mod.shape[2]

    def cs(shape):
        nd = len(shape)
        return pl.BlockSpec(shape, lambda *_: (0,) * nd, pipeline_mode=pl.Buffered(1))

    def bs(shape, fn):
        return pl.BlockSpec(shape, lambda i, t, *_: fn(i, t))

    in_specs = [
        bs((1, tm, d), lambda i, t: (i, t, 0)),
        bs((1, N_MOD, rm, d), lambda i, t: (i, 0, 0, 0)),
        cs((1, d)),
        cs(wts["w_in"].shape),
        cs((1, d_att)), cs((d_att, 1)), cs((MXU_DIM, MXU_DIM)),
        cs(wts["w_pg"].shape), cs((1, pw)), cs(wts["w_bp"].shape),
        bs((1, halo, pw), lambda i, t: (i, 0, 0)),
    ]
    out_specs = [
        bs((1, d_att, tm), lambda i, t: (i, 0, t)),
        bs((1, d_att, tm), lambda i, t: (i, 0, t)),
        bs((1, tm, d_att), lambda i, t: (i, t, 0)),
        bs((1, tm, d_att), lambda i, t: (i, t, 0)),
        bs((1, d_att, LANES), lambda i, t: (i, 0, 0)),
        bs((1, tm, d), lambda i, t: (i, t, 0)),
        bs((1, tm, d), lambda i, t: (i, t, 0)),
        bs((1, keep, pw), lambda i, t: (i, 0, 0)),
    ]
    out_shape = [
        jax.ShapeDtypeStruct((n, d_att, s), F32),
        jax.ShapeDtypeStruct((n, d_att, s), F32),
        jax.ShapeDtypeStruct((n, s, d_att), BF16),
        jax.ShapeDtypeStruct((n, s, d_att), BF16),
        jax.ShapeDtypeStruct((n, d_att, LANES), F32),
        jax.ShapeDtypeStruct((n, s, d), BF16),
        jax.ShapeDtypeStruct((n, s, d), BF16),
        jax.ShapeDtypeStruct((n, keep, pw), F32),
    ]
    n_lvl = max(len(POOL_WINDOWS) - 1, 1)
    scratch = [pltpu.VMEM((halo + tm, pw), F32), pltpu.VMEM((len(POOL_WINDOWS) * n_lvl, halo + tm, LANES), F32),
               pltpu.VMEM((2, d_att, d), BF16)]
    args = [x, mod, wts["g_mix"], wts["w_in"], wts["g_q"], wts["g_k"],
            wts["bd"], wts["w_pg"], wts["p_scale"], wts["w_bp"], state]
    n_prefetch = 0
    if stream is not None:
        gps, gpsq = stream[0], stream[1]
        in_specs.append(pl.BlockSpec(memory_space=pl.ANY))
        out_specs.append(bs((1, km_rows, LANES), lambda i, t: ((i * nt + t) * gps // gpsq, 0, 0)))
        out_shape.append(jax.ShapeDtypeStruct((n_seq, km_rows, LANES), F32))
        scratch += [pltpu.VMEM((2 * gps * _KM_GROUP, km_rows, page), F32), pltpu.SemaphoreType.DMA((2,))]
        args = [page_table] + args + [ck_t]
        n_prefetch = 1
    grid_spec = pltpu.PrefetchScalarGridSpec(num_scalar_prefetch=n_prefetch, grid=(n, nt), in_specs=in_specs,
                                             out_specs=out_specs, scratch_shapes=scratch)
    return pl.pallas_call(
        body, grid_spec=grid_spec, out_shape=out_shape,
        compiler_params=pltpu.CompilerParams(dimension_semantics=("arbitrary", "arbitrary"),
                                             vmem_limit_bytes=VMEM_LIMIT),
        name="mixer_inproj",
    )(*args)


def _attn_prompt_body(q_ref, k2_ref, kmt_ref, vt_ref, o_ref, va_ref, qh_ref, pen_ref, p_ref,
                      *, seq, nb, hd, n_pairs):
    blk = MOBA_BLOCK
    nbp = SUBLANES
    pw = 2 * hd
    lane_q = lax.broadcasted_iota(jnp.int32, (seq, pw), 1)
    lane_m = lax.broadcasted_iota(jnp.int32, (2 * SUBLANES, pw), 1)
    tpos = lax.broadcasted_iota(jnp.int32, (nbp, seq), 1)
    bidx = lax.broadcasted_iota(jnp.int32, (nbp, seq), 0)
    qblk = jnp.zeros((nbp, seq), jnp.int32)
    for b in range(1, nb):
        qblk = qblk + (tpos >= b * blk).astype(jnp.int32)
    ones_rows = (lax.broadcasted_iota(jnp.int32, (SUBLANES, seq), 0) == 0).astype(F32)
    kr = lax.broadcasted_iota(jnp.int32, (blk, blk), 0)
    qc = lax.broadcasted_iota(jnp.int32, (blk, blk), 1)
    causal = kr <= qc

    def setup_pieces(pair):
        ps = slice(pair * pw, (pair + 1) * pw)

        def values_piece(hh):
            rows = slice(pair * pw + hh * hd, pair * pw + (hh + 1) * hd)
            va = jnp.concatenate([vt_ref[0, rows, :], ones_rows, jnp.zeros((SUBLANES, seq), F32)], axis=0)
            va_ref[2 * (pair % 2) + hh] = va.astype(BF16)

        def select_piece(hh):
            q2 = q_ref[0, :, ps]
            km = jnp.transpose(kmt_ref[0, ps, :])
            in_head = (lane_m >= hh * hd) & (lane_m < (hh + 1) * hd)
            kmh = jnp.where(in_head, km[0:2 * SUBLANES], 0.0).astype(BF16)
            sct = _dot_nt(kmh, q2)[0:nbp]
            rank = jnp.zeros((nbp, seq), F32)
            for bp in range(nb):
                rowv = sct[bp:bp + 1, :]
                beats = (rowv > sct) | ((rowv == sct) & (bp < bidx))
                rank = rank + jnp.where(beats & (bp < qblk), 1.0, 0.0)
            keep = (bidx < qblk) & (rank < MOBA_TOPK)
            pen_ref[2 * (pair % 2) + hh] = jnp.where(keep, 0.0, NEG)
            own_lanes = (lane_q >= hh * hd) & (lane_q < (hh + 1) * hd)
            qh_ref[2 * (pair % 2) + hh] = jnp.where(own_lanes, q2, jnp.zeros_like(q2))

        return [functools.partial(values_piece, 0), functools.partial(values_piece, 1),
                functools.partial(select_piece, 0), functools.partial(select_piece, 1)]

    def main(pair, fillers):
        ps = slice(pair * pw, (pair + 1) * pw)
        items = [(i, hh) for i in range(nb) for hh in range(2)]
        m_of, zero_of, o_of, s_of = {}, {}, {}, {}

        def base_of(i):
            return blk * (i * (i + 1) // 2)

        def logits_piece(k, j):
            i, hh = items[k]
            qs = slice(i * blk, (i + 1) * blk)
            sv = _dot_nt(k2_ref[0, j * blk:(j + 1) * blk, ps], qh_ref[2 * (pair % 2) + hh, qs, :])
            s_of[k, j] = sv
            if j == i:
                mj = jnp.max(jnp.where(causal, sv, NEG), axis=0, keepdims=True)
            else:
                mj = jnp.max(sv, axis=0, keepdims=True) + pen_ref[2 * (pair % 2) + hh, j:j + 1, qs]
            m_of[k] = mj if j == 0 else jnp.maximum(m_of[k], mj)

        def exp_piece(k, j):
            i, hh = items[k]
            qs = slice(i * blk, (i + 1) * blk)
            if j == 0:
                zero_of[k] = [jnp.zeros((1, blk), F32)] * _EXP_DEPTH
            shift = m_of[k] if j == i else m_of[k] - pen_ref[2 * (pair % 2) + hh, j:j + 1, qs]
            sv = s_of.pop((k, j))
            for c in range(blk // _EXP_ROWS):
                r0 = base_of(i) + j * blk + c * _EXP_ROWS
                x = sv[c * _EXP_ROWS:(c + 1) * _EXP_ROWS]
                if j == i:
                    x = jnp.where(causal[c * _EXP_ROWS:(c + 1) * _EXP_ROWS], x, NEG)
                p = jnp.exp2(x - (shift + zero_of[k][c % _EXP_DEPTH]))
                p_ref[2 * (pair % 2) + hh, r0:r0 + _EXP_ROWS, :] = p.astype(BF16)
                zero_of[k][c % _EXP_DEPTH] = jnp.minimum(p[_EXP_ROWS - 1:_EXP_ROWS], 0.0)

        def values(k):
            i, hh = items[k]
            ln = (i + 1) * blk
            ot = _dot(va_ref[2 * (pair % 2) + hh, :, 0:ln], p_ref[2 * (pair % 2) + hh, base_of(i):base_of(i) + ln, :])
            o_of[k] = ot[0:hd] / ot[hd:hd + 1]
            if hh == 1:
                both = jnp.concatenate([o_of.pop(k - 1), o_of.pop(k)], axis=0)
                o_ref[0, i * blk:(i + 1) * blk, ps] = jnp.transpose(both).astype(BF16)

        for j in range(items[0][0] + 1):
            logits_piece(0, j)
        for k in range(len(items)):
            ahead = [(k + 1, j) for j in range(items[k + 1][0] + 1)] if k + 1 < len(items) else []
            for j in range(items[k][0] + 1):
                exp_piece(k, j)
                if ahead:
                    logits_piece(*ahead.pop(0))
            for kj in ahead:
                logits_piece(*kj)
            values(k)
            if fillers and k >= 2 and k % 2 == 0:
                fillers.pop(0)()
        for f in fillers:
            f()

    for piece in setup_pieces(0):
        piece()
    for pair in range(n_pairs):
        main(pair, setup_pieces(pair + 1) if pair + 1 < n_pairs else [])


def _attn_prompt_call(q, k2, kmt, vt, *, n_heads, hd):
    n, s, d_att = q.shape
    nb = s // MOBA_BLOCK
    n_pairs = n_heads // 2
    n_slot = 2 * min(n_pairs, 2)
    wd = n_pairs * 2 * hd
    body = functools.partial(_attn_prompt_body, seq=s, nb=nb, hd=hd, n_pairs=n_pairs)
    tri = MOBA_BLOCK * (nb * (nb + 1) // 2)
    tok_spec = pl.BlockSpec((1, s, wd), lambda i, p: (i, 0, p))
    return pl.pallas_call(
        body, grid=(n, d_att // wd),
        in_specs=[tok_spec, tok_spec,
                  pl.BlockSpec((1, wd, LANES), lambda i, p: (i, p, 0)),
                  pl.BlockSpec((1, wd, s), lambda i, p: (i, p, 0))],
        out_specs=tok_spec,
        out_shape=jax.ShapeDtypeStruct((n, s, d_att), BF16),
        scratch_shapes=[pltpu.VMEM((n_slot, hd + 2 * SUBLANES, s), BF16),
                        pltpu.VMEM((n_slot, s, 2 * hd), BF16), pltpu.VMEM((n_slot, SUBLANES, s), F32),
                        pltpu.VMEM((n_slot, tri, MOBA_BLOCK), BF16)],
        compiler_params=pltpu.CompilerParams(dimension_semantics=("arbitrary", "arbitrary"),
                                             vmem_limit_bytes=VMEM_LIMIT),
        name="moba_prompt",
    )(q, k2, kmt, vt)


def _post_body(x_ref, o_ref, ag_ref, gb_ref, mod_ref, wba_ref, wout_ref, gffn_ref, wup_ref, wconv_ref, bconv_ref,
               wdown_ref, cst_ref, y_ref, conv_ref, gbuf, act_ref, *, tm, rows_per_tok, halo, nt, pc, down_groups, d_ff):
    t = pl.program_id(1)
    x = x_ref[0]
    gt_m = mod_ref[0, 2]
    sh_f = mod_ref[0, 3]
    sc_f = mod_ref[0, 4]
    gt_f = mod_ref[0, 5]
    oa = _dot(o_ref[0], wba_ref[...])
    merged = ag_ref[0].astype(F32) + gb_ref[0].astype(F32) * oa
    x1 = x + gt_m * _dot(merged.astype(BF16), wout_ref[...])
    ms = jnp.mean(x1 * x1, axis=-1, keepdims=True)
    h2 = x1 * lax.rsqrt(ms + EPS) * gffn_ref[...]
    h2 = (h2 * (1.0 + sc_f) + sh_f).astype(BF16)

    @pl.when(t == 0)
    def _():
        gbuf[0:halo, :] = cst_ref[0]

    @pl.when(t > 0)
    def _():
        gbuf[0:halo, :] = gbuf[tm:tm + halo, :]

    r = rows_per_tok
    n_pc = d_ff // pc

    def up(c):
        cs = slice(c * pc, (c + 1) * pc)
        fg = _dot(h2, wup_ref[:, cs])
        fv = _dot(h2, wup_ref[:, d_ff + c * pc:d_ff + (c + 1) * pc])
        gbuf[halo:halo + tm, cs] = fg
        return fg, fv

    def activate(c, fg, fv):
        cs = slice(c * pc, (c + 1) * pc)
        conv = bconv_ref[:, cs]
        conv = conv + gbuf[halo - 2 * r:halo - 2 * r + tm, cs] * wconv_ref[0:1, cs]
        conv = conv + gbuf[halo - r:halo - r + tm, cs] * wconv_ref[1:2, cs]
        conv = conv + fg * wconv_ref[2:3, cs]
        act_ref[:, cs] = (conv * _sigmoid(conv) * fv).astype(BF16)

    bounds = [round(g * n_pc / down_groups) * pc for g in range(down_groups + 1)]
    acc = None
    nxt = up(0)
    for c in range(n_pc):
        cur = nxt
        if c + 1 < n_pc:
            nxt = up(c + 1)
        activate(c, *cur)
        if (c + 1) * pc in bounds[1:]:
            lo = bounds[bounds.index((c + 1) * pc) - 1]
            part = _dot(act_ref[:, lo:(c + 1) * pc], wdown_ref[lo:(c + 1) * pc, :])
            acc = part if acc is None else acc + part
    y_ref[0] = x1 + gt_f * acc

    @pl.when(t == nt - 1)
    def _():
        keep = (CONV_W - 1) * r
        conv_ref[0] = gbuf[halo + tm - keep:halo + tm, :]


def _post_call(x, o, ag, gb, mod, cstate, wts, *, tm, rows_per_tok, halo, pc, down_groups):
    n, s, d = x.shape
    nt = s // tm
    d_att = o.shape[-1]
    d_ff = wts["w_down"].shape[0]
    keep = (CONV_W - 1) * rows_per_tok
    rm = mod.shape[2]
    body = functools.partial(_post_body, tm=tm, rows_per_tok=rows_per_tok, halo=halo, nt=nt, pc=pc,
                             down_groups=down_groups, d_ff=d_ff)
    row_spec = lambda w: pl.BlockSpec((1, tm, w), lambda i, t: (i, t, 0))
    in_specs = [
        row_spec(d), row_spec(d_att), row_spec(d), row_spec(d),
        pl.BlockSpec((1, N_MOD, rm, d), lambda i, t: (i, 0, 0, 0)),
        _const_spec(wts["w_ba"].shape), _const_spec(wts["w_out"].shape), _const_spec((1, d)),
        _const_spec(wts["w_up"].shape), _const_spec((CONV_W, d_ff)), _const_spec((1, d_ff)),
        _const_spec(wts["w_down"].shape),
        pl.BlockSpec((1, halo, d_ff), lambda i, t: (i, 0, 0)),
    ]
    out_specs = [row_spec(d), pl.BlockSpec((1, keep, d_ff), lambda i, t: (i, 0, 0))]
    out_shape = [jax.ShapeDtypeStruct((n, s, d), F32), jax.ShapeDtypeStruct((n, keep, d_ff), F32)]
    return pl.pallas_call(
        body, grid=(n, nt), in_specs=in_specs, out_specs=out_specs, out_shape=out_shape,
        scratch_shapes=[pltpu.VMEM((halo + tm, d_ff), F32), pltpu.VMEM((tm, d_ff), BF16)],
        compiler_params=pltpu.CompilerParams(dimension_semantics=("arbitrary", "arbitrary"),
                                             vmem_limit_bytes=VMEM_LIMIT),
        name="merge_ffn",
    )(x, o, ag, gb, mod, wts["w_ba"], wts["w_out"], wts["g_ffn"], wts["w_up"], wts["w_conv"], wts["b_conv"],
      wts["w_down"], cstate)


def _kmean_body(pt_ref, ck_ref, km_ref, buf, sem, *, n_seq, n_pages, ppb):
    n = pl.program_id(0)
    rows = km_ref.shape[1]
    bpg = _KM_GROUP // ppb
    n_groups = n_pages // _KM_GROUP

    def copy(seq, p, slot):
        return pltpu.make_async_copy(ck_ref.at[pt_ref[seq, p]], buf.at[slot], sem.at[slot])

    def start_group(seq, g, half):
        for k in range(_KM_GROUP):
            copy(seq, g * _KM_GROUP + k, half * _KM_GROUP + k).start()

    @pl.when(n == 0)
    def _():
        start_group(0, 0, 0)
        start_group(0, 1, 1)

    km_ref[0] = jnp.zeros((rows, LANES), F32)
    lane = lax.broadcasted_iota(jnp.int32, (_KM_ROWS, LANES), 1)

    def pair(g2, carry):
        for half in range(2):
            g = g2 * 2 + half
            for k in range(_KM_GROUP):
                copy(n, g * _KM_GROUP + k, half * _KM_GROUP + k).wait()
            for rc in range(rows // _KM_ROWS):
                rs = slice(rc * _KM_ROWS, (rc + 1) * _KM_ROWS)
                upd = jnp.zeros((_KM_ROWS, LANES), F32)
                for bb in range(bpg):
                    slot = half * _KM_GROUP + bb * ppb
                    x = buf[slot, rs, :]
                    for pp in range(1, ppb):
                        x = x + buf[slot + pp, rs, :]
                    col = jnp.sum(x, axis=-1, keepdims=True)
                    upd = jnp.where(lane == g * bpg + bb, col, upd)
                km_ref[0, rs, :] = km_ref[0, rs, :] + upd * (1.0 / MOBA_BLOCK)
            nxt = g + 2

            @pl.when(nxt < n_groups)
            def _():
                start_group(n, nxt, half)

            @pl.when((nxt >= n_groups) & (n + 1 < n_seq))
            def _():
                start_group(n + 1, nxt - n_groups, half)
        return carry

    lax.fori_loop(0, n_groups // 2, pair, 0)


def _kmean_call(page_table, ck_t):
    n, n_pages = page_table.shape
    _, rows, page = ck_t.shape
    ppb = MOBA_BLOCK // page
    assert n_pages % (2 * _KM_GROUP) == 0 and rows % _KM_ROWS == 0
    body = functools.partial(_kmean_body, n_seq=n, n_pages=n_pages, ppb=ppb)
    grid_spec = pltpu.PrefetchScalarGridSpec(
        num_scalar_prefetch=1, grid=(n,),
        in_specs=[pl.BlockSpec(memory_space=pl.ANY)],
        out_specs=pl.BlockSpec((1, rows, LANES), lambda i, pt: (i, 0, 0)),
        scratch_shapes=[pltpu.VMEM((2 * _KM_GROUP, rows, page), F32), pltpu.SemaphoreType.DMA((2 * _KM_GROUP,))],
    )
    return pl.pallas_call(
        body, grid_spec=grid_spec,
        out_shape=jax.ShapeDtypeStruct((n, rows, LANES), F32),
        compiler_params=pltpu.CompilerParams(dimension_semantics=("arbitrary",), vmem_limit_bytes=VMEM_LIMIT),
        name="cache_block_means",
    )(page_table, ck_t)


def _topk_body(q_ref, km_ref, sel_ref, *, n_heads, hd, n_blk, top, unroll):
    n_seq = q_ref.shape[0]
    rows = n_heads * SUBLANES
    lane_q = lax.broadcasted_iota(jnp.int32, (SUBLANES, n_heads * hd), 1)
    lane = lax.broadcasted_iota(jnp.int32, (unroll * rows, LANES), 1).astype(F32)

    def per_group(ig, carry):
        scs = []
        for k in range(unroll):
            i = ig * unroll + k
            q8 = q_ref[i]
            qbd = jnp.concatenate(
                [jnp.where((lane_q >= hh * hd) & (lane_q < (hh + 1) * hd), q8, 0.0) for hh in range(n_heads)], axis=0)
            scs.append(_dot(qbd.astype(BF16), km_ref[i].astype(BF16)))
        sc = jnp.where(lane < n_blk, jnp.concatenate(scs, axis=0), -jnp.inf)
        out = jnp.zeros((unroll * rows, LANES), F32)
        for r in range(top):
            m = jnp.max(sc, axis=-1, keepdims=True)
            idx = jnp.min(jnp.where(sc == m, lane, float(LANES)), axis=-1, keepdims=True)
            out = jnp.where(lane == r, idx, out)
            sc = jnp.where(lane == idx, -jnp.inf, sc)
        out = out.astype(jnp.int32)
        for k in range(unroll):
            sel_ref[ig * unroll + k] = out[k * rows:(k + 1) * rows]
        return carry

    lax.fori_loop(0, n_seq // unroll, per_group, 0)


def _topk_call(q8, kmt, *, n_heads, hd, n_blk, top):
    n = q8.shape[0]
    rows = n_heads * SUBLANES
    unroll = max(u for u in (8, 4, 2, 1) if n % u == 0)
    body = functools.partial(_topk_body, n_heads=n_heads, hd=hd, n_blk=n_blk, top=top, unroll=unroll)
    return pl.pallas_call(
        body, grid=(1,),
        in_specs=[pl.BlockSpec(q8.shape, lambda i: (0, 0, 0)), pl.BlockSpec(kmt.shape, lambda i: (0, 0, 0))],
        out_specs=pl.BlockSpec((n, rows, LANES), lambda i: (0, 0, 0)),
        out_shape=jax.ShapeDtypeStruct((n, rows, LANES), jnp.int32),
        compiler_params=pltpu.CompilerParams(vmem_limit_bytes=VMEM_LIMIT),
        name="block_topk",
    )(q8, kmt)


def _attn_sample_body(pt_ref, sel_ref, qt_ref, knt_ref, vnt_ref, lseq_ref, ltok_ref, ck_ref, cv_ref, o_ref, kbuf, vbuf,
                      lg_ref, p_ref, sem,
                      *, n_seq, n_heads, hd, n_tok, top, ppb):
    n = pl.program_id(0)
    n_sel = top * ppb
    n_grp = n_heads * n_tok
    slot = lax.rem(n, 2)

    def gather(seq, sl):
        for hh in range(n_heads):
            for s in range(n_tok):
                g = hh * n_tok + s
                for j in range(n_sel):
                    r, pp = divmod(j, ppb)
                    page = pt_ref[seq, sel_ref[seq, g * top + r] * ppb + pp]
                    pltpu.make_async_copy(ck_ref.at[page, hh], kbuf.at[sl, g, j], sem.at[sl]).start()
                    pltpu.make_async_copy(cv_ref.at[page, hh], vbuf.at[sl, g, j], sem.at[sl]).start()

    @pl.when(n == 0)
    def _():
        gather(0, 0)

    @pl.when(n + 1 < n_seq)
    def _():
        gather(n + 1, 1 - slot)

    for g in range(n_grp):
        for j in range(n_sel):
            pltpu.make_async_copy(ck_ref.at[0, 0], kbuf.at[slot, g, j], sem.at[slot]).wait()
            pltpu.make_async_copy(cv_ref.at[0, 0], vbuf.at[slot, g, j], sem.at[slot]).wait()

    mine = lseq_ref[...] == n
    lane_tok = ltok_ref[...]
    pad_rows = [jnp.full((1, LANES), NEG, F32)] * (SUBLANES - n_sel - 1)
    for hh in range(n_heads):
        rs = slice(hh * hd, (hh + 1) * hd)
        knt = knt_ref[rs, :]
        for s in range(n_tok):
            g = hh * n_tok + s
            qb = jnp.broadcast_to(qt_ref[0, rs, s:s + 1], (hd, LANES))
            rows = [jnp.sum(kbuf[slot, g, j] * qb, axis=0, keepdims=True) for j in range(n_sel)]
            own = jnp.sum(knt * qb, axis=0, keepdims=True)
            rows.append(jnp.where(mine & (lane_tok <= s), own, NEG))
            lg_ref[g] = jnp.concatenate(rows + pad_rows, axis=0)

    lg = lg_ref[...]
    m = jnp.max(jnp.max(lg, axis=1), axis=-1, keepdims=True)
    mb = jnp.broadcast_to(m, (n_grp, LANES))
    for g in range(n_grp):
        p_ref[g] = jnp.exp2(lg_ref[g] - mb[g:g + 1, :])
    l = jnp.sum(jnp.sum(p_ref[...], axis=1), axis=-1, keepdims=True)
    inv = jnp.broadcast_to(1.0 / l, (n_grp, LANES))

    lane_o = lax.broadcasted_iota(jnp.int32, (hd, LANES), 1)
    outs = []
    for hh in range(n_heads):
        rs = slice(hh * hd, (hh + 1) * hd)
        vnt = vnt_ref[rs, :]
        out_h = jnp.zeros((hd, LANES), F32)
        for s in range(n_tok):
            g = hh * n_tok + s
            pg = p_ref[g] * inv[g:g + 1, :]
            acc = vnt * pg[n_sel:n_sel + 1, :]
            for j in range(n_sel):
                acc = acc + vbuf[slot, g, j] * pg[j:j + 1, :]
            col = jnp.sum(acc, axis=-1, keepdims=True)
            out_h = jnp.where(lane_o == s, col, out_h)
        outs.append(out_h)
    o_ref[0] = jnp.transpose(jnp.concatenate(outs, axis=0))[0:SUBLANES]


def _attn_sample_call(page_table, sel, qt, knt, vnt, lane_seq, lane_tok, ck_t, cv_t, *, n_heads, hd, n_tok, top):
    n = page_table.shape[0]
    page = ck_t.shape[-1]
    ppb = MOBA_BLOCK // page
    d_att = n_heads * hd
    n_grp = n_heads * n_tok
    assert top * ppb + 1 <= SUBLANES
    body = functools.partial(_attn_sample_body, n_seq=n, n_heads=n_heads, hd=hd, n_tok=n_tok, top=top, ppb=ppb)
    blk = pl.BlockSpec((1, d_att, LANES), lambda i, pt, sl: (i, 0, 0))
    whole = lambda shape: pl.BlockSpec(shape, lambda i, pt, sl: (0,) * len(shape))
    grid_spec = pltpu.PrefetchScalarGridSpec(
        num_scalar_prefetch=2, grid=(n,),
        in_specs=[blk, whole((d_att, LANES)), whole((d_att, LANES)), whole((1, LANES)), whole((1, LANES)),
                  pl.BlockSpec(memory_space=pl.ANY), pl.BlockSpec(memory_space=pl.ANY)],
        out_specs=pl.BlockSpec((1, SUBLANES, d_att), lambda i, pt, sl: (i, 0, 0)),
        scratch_shapes=[pltpu.VMEM((2, n_grp, top * ppb, hd, page), F32),
                        pltpu.VMEM((2, n_grp, top * ppb, hd, page), F32),
                        pltpu.VMEM((n_grp, SUBLANES, LANES), F32),
                        pltpu.VMEM((n_grp, SUBLANES, LANES), F32),
                        pltpu.SemaphoreType.DMA((2,))],
    )
    return pl.pallas_call(
        body, grid_spec=grid_spec,
        out_shape=jax.ShapeDtypeStruct((n, SUBLANES, d_att), F32),
        compiler_params=pltpu.CompilerParams(dimension_semantics=("arbitrary",), vmem_limit_bytes=VMEM_LIMIT),
        name="moba_sample",
    )(page_table, sel, qt, knt, vnt, lane_seq, lane_tok, ck_t, cv_t)


def _layer_weights(l, g_norm_mix, w_in, g_q, g_k, w_pool_group, pool_scale, w_branch_pool, w_branch_attn, w_out,
                   g_norm_ffn, w_up, w_conv, b_conv, w_down, n_heads, hd):
    d = w_in.shape[1]
    pw = w_pool_group.shape[1] * w_pool_group.shape[2]
    d_att = n_heads * hd
    wi = w_in[l].astype(BF16)
    blk = jnp.arange(MXU_DIM) // hd
    return {
        "g_mix": g_norm_mix[l].reshape(1, d),
        "w_in": wi,
        "g_q": g_q[l].reshape(1, d_att), "g_k": g_k[l].reshape(d_att, 1),
        "bd": jnp.where(blk[:, None] == blk[None, :], 1.0 / hd, 0.0).astype(BF16),
        "w_pg": w_pool_group[l].astype(BF16), "p_scale": pool_scale[l].reshape(1, pw),
        "w_bp": w_branch_pool[l].astype(BF16),
        "w_ba": w_branch_attn[l].astype(BF16), "w_out": w_out[l].astype(BF16),
        "g_ffn": g_norm_ffn[l].reshape(1, d), "w_up": w_up[l].astype(BF16),
        "w_conv": w_conv[l], "b_conv": b_conv[l].reshape(1, -1), "w_down": w_down[l].astype(BF16),
    }


def kernel(x_prompt, x_sample, cache_k, cache_v, state_pool, state_ffn_conv, page_table, c_prompt, c_sample, w_ada, b_ada, g_norm_mix, w_in, g_q, g_k, w_pool_group, pool_scale, w_branch_pool, w_branch_attn, w_out, g_norm_ffn, w_up, w_conv, b_conv, w_down):
    nb_p, seq, d = x_prompt.shape
    nb_s, n_tok, _ = x_sample.shape
    depth, n_phys, page, n_heads, hd = cache_k.shape
    d_att = n_heads * hd
    pw = state_pool.shape[-1]
    d_ff = w_down.shape[1]
    n_pages = page_table.shape[1]
    past = n_pages * page
    assert MOBA_BLOCK % page == 0 and past % MOBA_BLOCK == 0 and seq % MOBA_BLOCK == 0
    assert n_tok <= SUBLANES and nb_s * n_tok <= LANES
    n_blk = past // MOBA_BLOCK
    top = min(MOBA_TOPK, n_blk)
    assert 0 < n_blk <= LANES and seq // MOBA_BLOCK <= SUBLANES
    tm = 256
    tm_ffn = 512 if seq % 512 == 0 else tm
    pc, dg = MXU_DIM, 3
    assert d_ff % pc == 0
    rows_s = nb_s * n_tok

    y_p, y_s = x_prompt, x_sample.transpose(1, 0, 2).reshape(1, rows_s, d)
    halo_p = 4 * SUBLANES
    zero_pool = jnp.zeros((nb_p, halo_p, pw), F32)
    zero_conv = jnp.zeros((nb_p, SUBLANES, d_ff), F32)
    outs = [[] for _ in range(8)]
    for l in range(depth):
        wts = _layer_weights(l, g_norm_mix, w_in, g_q, g_k, w_pool_group, pool_scale, w_branch_pool, w_branch_attn,
                             w_out, g_norm_ffn, w_up, w_conv, b_conv, w_down, n_heads, hd)
        mod = _ada_call(jnp.concatenate([c_prompt, c_sample], axis=0), w_ada[l].astype(BF16), b_ada[l])
        mod_p = mod[:nb_p].reshape(nb_p, N_MOD, 1, d)
        mod_s = jnp.tile(mod[nb_p:].reshape(nb_s, N_MOD, d), (n_tok, 1, 1)).transpose(1, 0, 2)[None]

        ck_t = cache_k[l].transpose(0, 2, 3, 1)
        cv_t = cache_v[l].transpose(0, 2, 3, 1)
        ck_flat = ck_t.reshape(n_phys, d_att, page)
        gpsq, steps = n_pages // _KM_GROUP, nb_p * (seq // tm)
        gps = nb_s * gpsq // steps
        fuse = n_pages % _KM_GROUP == 0 and gps >= 1 and gps * steps == nb_s * gpsq and gpsq % gps == 0
        res = _inproj_call(y_p, mod_p, zero_pool, wts, tm=tm, rows_per_tok=1, halo=halo_p, pos0=0,
                           n_heads=n_heads, hd=hd, block_means=True, pages=(page_table, ck_flat) if fuse else None)
        kt, vt, q, k2, kmt, ag, gb, pool_p = res[:8]
        kmt_s = res[8] if fuse else _kmean_call(page_table, ck_flat)
        o = _attn_prompt_call(q, k2, kmt, vt, n_heads=n_heads, hd=hd)
        y_p, conv_p = _post_call(y_p, o, ag, gb, mod_p, zero_conv, wts, tm=tm_ffn, rows_per_tok=1, halo=SUBLANES, pc=pc,
                                 down_groups=dg)
        to_rows = lambda a: a.reshape(nb_p, n_heads, hd, seq).transpose(0, 3, 1, 2)
        outs[0].append(to_rows(kt)); outs[1].append(to_rows(vt)); outs[2].append(pool_p); outs[3].append(conv_p)

        pool_state = state_pool[l].transpose(1, 0, 2).reshape(1, POOL_CTX * nb_s, pw)
        conv_state = state_ffn_conv[l].transpose(1, 0, 2).reshape(1, (CONV_W - 1) * nb_s, d_ff)
        kt_s, vt_s, q_s, _, _, ag_s, gb_s, pool_s = _inproj_call(
            y_s, mod_s, pool_state, wts, tm=rows_s, rows_per_tok=nb_s, halo=POOL_CTX * nb_s, pos0=past,
            n_heads=n_heads, hd=hd, block_means=False)
        q_ns = q_s.reshape(n_tok, nb_s, d_att).transpose(1, 0, 2).astype(F32)
        q8 = jnp.pad(q_ns, ((0, 0), (0, SUBLANES - n_tok), (0, 0)))
        sel = _topk_call(q8, kmt_s, n_heads=n_heads, hd=hd, n_blk=n_blk, top=top)
        sel = sel.reshape(nb_s, n_heads, SUBLANES, LANES)[:, :, :n_tok, :top].reshape(nb_s, n_heads * n_tok * top)
        qt = jnp.pad(q_ns.transpose(0, 2, 1), ((0, 0), (0, 0), (0, LANES - n_tok)))
        lane_pad = lambda a: jnp.pad(a, ((0, 0), (0, LANES - rows_s)))
        lanes = jnp.arange(LANES)
        lane_seq = jnp.where(lanes < rows_s, lanes % nb_s, -1).astype(jnp.int32).reshape(1, LANES)
        lane_tok = (lanes // nb_s).astype(jnp.int32).reshape(1, LANES)
        ot = _attn_sample_call(page_table, sel, qt, lane_pad(kt_s[0]), lane_pad(vt_s[0]), lane_seq, lane_tok, ck_t, cv_t,
                               n_heads=n_heads, hd=hd, n_tok=n_tok, top=top)
        o_s = ot[:, :n_tok].transpose(1, 0, 2).reshape(1, rows_s, d_att).astype(BF16)
        y_s, conv_s = _post_call(y_s, o_s, ag_s, gb_s, mod_s, conv_state, wts, tm=rows_s, rows_per_tok=nb_s,
                                 halo=(CONV_W - 1) * nb_s, pc=pc, down_groups=dg)
        to_rows_s = lambda a: a[0].reshape(n_heads, hd, n_tok, nb_s).transpose(3, 2, 0, 1)
        outs[4].append(to_rows_s(kt_s)); outs[5].append(to_rows_s(vt_s))
        outs[6].append(pool_s.reshape(POOL_CTX, nb_s, pw).transpose(1, 0, 2))
        outs[7].append(conv_s.reshape(CONV_W - 1, nb_s, d_ff).transpose(1, 0, 2))

    y_s_out = y_s.reshape(n_tok, nb_s, d).transpose(1, 0, 2)
    st = [jnp.stack(v) for v in outs]
    return (y_p, y_s_out, st[0], st[1], st[2], st[3], st[4], st[5], st[6], st[7])
```

```python
import functools

import jax
import jax.numpy as jnp
from jax import lax
from jax.experimental import pallas as pl
from jax.experimental.pallas import tpu as pltpu

F32 = jnp.float32
BF16 = jnp.bfloat16

MOBA_BLOCK = 256
MOBA_TOPK = 3
POOL_WINDOWS = (2, 4, 8, 16)
POOL_CTX = max(POOL_WINDOWS) - 1
CONV_W = 3
N_MOD = 6
EPS = 1e-6
NEG = -1e30
LOG2E = 1.4426950408889634

LANES = 128
SUBLANES = 8
MXU_DIM = 256
VMEM_LIMIT = 60 * 1024 * 1024
_EXP_ROWS = 64
_EXP_DEPTH = 1
_KM_GROUP = 16
_KM_ROWS = 64


def _const_spec(shape):
    nd = len(shape)
    return pl.BlockSpec(shape, lambda *_: (0,) * nd, pipeline_mode=pl.Buffered(1))


def _dot(a, b):
    return jnp.dot(a, b, preferred_element_type=F32)


def _dot_nt(a, b):
    return lax.dot_general(a, b, (((1,), (1,)), ((), ())), preferred_element_type=F32)


def _sigmoid(x):
    return 1.0 / (1.0 + jnp.exp(-x))


def _ada_body(c_ref, w_ref, b_ref, o_ref):
    c = c_ref[...]
    a = (c * _sigmoid(c)).astype(BF16)
    o_ref[...] = _dot(a, w_ref[...].astype(BF16)) + b_ref[...]


def _ada_call(c, w_ada, b_ada):
    n, d = c.shape
    dm = w_ada.shape[1]
    bn = d
    return pl.pallas_call(
        _ada_body,
        grid=(dm // bn,),
        in_specs=[pl.BlockSpec((n, d), lambda j: (0, 0)),
                  pl.BlockSpec((d, bn), lambda j: (0, j)),
                  pl.BlockSpec((1, bn), lambda j: (0, j))],
        out_specs=pl.BlockSpec((n, bn), lambda j: (0, j)),
        out_shape=jax.ShapeDtypeStruct((n, dm), F32),
        name="ada_mod",
    )(c, w_ada, b_ada.reshape(1, dm))


def _inproj_body(*refs, tm, rows_per_tok, halo, pos0, nt, n_heads, hd, scale, block_means, stream):
    if stream is not None:
        pt_ref, refs = refs[0], refs[1:]
    (x_ref, mod_ref, gmix_ref, win_ref, gq_ref, gk_ref, bd_ref, wpg_ref, psc_ref,
     wbp_ref, st_ref) = refs[:11]
    refs = refs[11:]
    if stream is not None:
        ck_ref, refs = refs[0], refs[1:]
    kt_ref, vt_ref, q_ref, k2_ref, kmt_ref, ag_ref, gb_ref, pool_ref = refs[:8]
    refs = refs[8:]
    if stream is not None:
        kms_ref, ubuf, wbuf, wkv_t, pbuf, psem = refs
    else:
        ubuf, wbuf, wkv_t = refs
    n_lvl = max(len(POOL_WINDOWS) - 1, 1)
    t = pl.program_id(1)
    d_att = n_heads * hd
    d_model = x_ref.shape[-1]
    cw = MXU_DIM

    todo = []
    if stream is not None:
        gps, gpsq, total, ppb = stream
        bpg = _KM_GROUP // ppb
        step = pl.program_id(0) * nt + t
        n_steps = total // gps
        km_rows = kms_ref.shape[1]
        lane_s = lax.broadcasted_iota(jnp.int32, (_KM_ROWS, LANES), 1)
        side_set = lax.rem(step, 2)

        def batch_copies(b, sset):
            out = []
            for k in range(gps):
                gidx = b * gps + k
                seq, g = lax.div(gidx, gpsq), lax.rem(gidx, gpsq)
                for j in range(_KM_GROUP):
                    slot = (sset * gps + k) * _KM_GROUP + j
                    out.append(pltpu.make_async_copy(ck_ref.at[pt_ref[seq, g * _KM_GROUP + j]], pbuf.at[slot],
                                                     psem.at[sset]))
            return out

        @pl.when(step == 0)
        def _():
            for b0 in range(min(2, n_steps)):
                for cp in batch_copies(jnp.int32(b0), b0):
                    cp.start()

        for slot in range(gps * _KM_GROUP):
            pltpu.make_async_copy(ck_ref.at[0], pbuf.at[side_set * gps * _KM_GROUP + slot], psem.at[side_set]).wait()

        @pl.when(lax.rem(step * gps, gpsq) == 0)
        def _():
            kms_ref[0] = jnp.zeros((km_rows, LANES), F32)

        def stream_reduce(k, rc):
            g = lax.rem(step * gps + k, gpsq)
            rs = slice(rc * _KM_ROWS, (rc + 1) * _KM_ROWS)
            upd = jnp.zeros((_KM_ROWS, LANES), F32)
            for bb in range(bpg):
                slot = (side_set * gps + k) * _KM_GROUP + bb * ppb
                xs = pbuf[slot, rs, :]
                for pp in range(1, ppb):
                    xs = xs + pbuf[slot + pp, rs, :]
                upd = jnp.where(lane_s == g * bpg + bb, jnp.sum(xs, axis=-1, keepdims=True), upd)
            kms_ref[0, rs, :] = kms_ref[0, rs, :] + upd * (1.0 / MOBA_BLOCK)

        todo = [(k, rc) for k in range(gps) for rc in range(km_rows // _KM_ROWS)]

    def side(n_pieces=1):
        for _ in range(n_pieces):
            if todo:
                stream_reduce(*todo.pop(0))

    pw = wpg_ref.shape[0] * wpg_ref.shape[2]
    q0, k0, v0, g0 = pw, pw + d_att, pw + 2 * d_att, pw + 3 * d_att

    @pl.when((pl.program_id(0) == 0) & (t == 0))
    def _():
        wkv_t[0] = jnp.transpose(win_ref[:, k0:v0].astype(F32)).astype(BF16)
        wkv_t[1] = jnp.transpose(win_ref[:, v0:g0].astype(F32)).astype(BF16)

    x = x_ref[0]
    sh_m = mod_ref[0, 0]
    sc_m = mod_ref[0, 1]
    ms = jnp.mean(x * x, axis=-1, keepdims=True)
    h = x * lax.rsqrt(ms + EPS) * gmix_ref[...]
    h = (h * (1.0 + sc_m) + sh_m).astype(BF16)

    u = jnp.concatenate([_dot(h, win_ref[:, c * cw:(c + 1) * cw]) for c in range(pw // cw)], axis=1)

    @pl.when(t == 0)
    def _():
        ubuf[0:halo, :] = st_ref[0]

    @pl.when(t > 0)
    def _():
        ubuf[0:halo, :] = ubuf[tm:tm + halo, :]

    ubuf[halo:halo + tm, :] = u
    gw = LANES

    def window(g):
        w = POOL_WINDOWS[g]
        cs = slice(g * gw, (g + 1) * gw)
        n_rows = halo + tm
        src, valid_from, lvl, d = None, 0, 0, 1
        while 2 * d < w:
            s = d * rows_per_tok
            a = -(-s // SUBLANES) * SUBLANES
            dst = wbuf.at[g * n_lvl + lvl]
            if src is None:
                dst[a:n_rows, :] = ubuf[a:n_rows, cs] + ubuf[a - s:n_rows - s, cs]
            else:
                dst[a:n_rows, :] = src[a:n_rows, :] + src[a - s:n_rows - s, :]
            valid_from = max(a, valid_from + s)
            src, lvl, d = dst, lvl + 1, 2 * d
        s = d * rows_per_tok
        assert halo - s >= valid_from
        if src is None:
            acc = ubuf[halo:n_rows, cs] + ubuf[halo - s:n_rows - s, cs]
        else:
            acc = src[halo:n_rows, :] + src[halo - s:n_rows - s, :]
        if pos0 >= w - 1:
            cnt = float(w)
        else:
            assert rows_per_tok == 1
            row = lax.broadcasted_iota(jnp.int32, (tm, 1), 0) + (t * tm + pos0 + 1)
            cnt = jnp.minimum(float(w), row.astype(F32))
        return (acc / cnt - u[:, cs]).astype(BF16)

    def key_heads(kt_piece, c):
        outs = []
        for hh in range(cw // hd):
            rs = slice(c * cw + hh * hd, c * cw + (hh + 1) * hd)
            kh = kt_piece[hh * hd:(hh + 1) * hd]
            msk = jnp.sum(kh * kh, axis=0, keepdims=True) * (1.0 / hd)
            outs.append(kh * lax.rsqrt(msk + EPS) * gk_ref[rs])
            kt_ref[0, rs, :] = outs[-1]
        return jnp.concatenate(outs, axis=0)

    n_kc = d_att // cw
    pooled, kn = [], []
    for c in range(n_kc):
        ktc = _dot_nt(wkv_t[0, c * cw:(c + 1) * cw, :], h)
        pooled.append(window(c))
        side()
        kn.append((ktc, c))
    for c in range(n_kc):
        vt_ref[0, c * cw:(c + 1) * cw, :] = _dot_nt(wkv_t[1, c * cw:(c + 1) * cw, :], h)
        kn[c] = key_heads(*kn[c])
        side()
    kn = jnp.concatenate(kn, axis=0)
    qs = []
    for c in range(n_kc):
        qs.append(_dot(h, win_ref[:, q0 + c * cw:q0 + (c + 1) * cw]))
        if n_kc + c < len(POOL_WINDOWS):
            pooled.append(window(n_kc + c))
        side()
    for g in range(len(pooled), len(POOL_WINDOWS)):
        pooled.append(window(g))
    k2_ref[0] = jnp.transpose(kn).astype(BF16)

    @pl.when(t == 0)
    def _():
        kmt_ref[0] = jnp.zeros((d_att, LANES), F32)

    if block_means:
        lane_b = lax.broadcasted_iota(jnp.int32, (d_att, LANES), 1)
        for bb in range(tm // MOBA_BLOCK):
            part = kn[:, bb * MOBA_BLOCK:bb * MOBA_BLOCK + LANES]
            for c in range(1, MOBA_BLOCK // LANES):
                part = part + kn[:, bb * MOBA_BLOCK + c * LANES:bb * MOBA_BLOCK + (c + 1) * LANES]
            col = jnp.sum(part, axis=-1, keepdims=True) * (1.0 / MOBA_BLOCK)
            kmt_ref[0] = jnp.where(lane_b == t * (tm // MOBA_BLOCK) + bb, col, kmt_ref[0])

    ys = [_dot(pooled[g], wpg_ref[g]) for g in range(len(POOL_WINDOWS))]
    split = []
    for c in range(n_kc):
        qq = qs[c] * qs[c]
        hi = qq.astype(BF16)
        split.append((hi, (qq - hi.astype(F32)).astype(BF16)))
    side()
    y = (jnp.concatenate(ys, axis=1) * psc_ref[...]).astype(BF16)
    a = []
    for c in range(d_model // cw):
        a.append(_dot(y, wbp_ref[:, c * cw:(c + 1) * cw]))
        if c < n_kc:
            msq = _dot(split[c][0], bd_ref[...]) + _dot(split[c][1], bd_ref[...])
            cs = slice(c * cw, (c + 1) * cw)
            q_ref[0, :, cs] = (qs[c] * lax.rsqrt(msq + EPS) * gq_ref[:, cs] * scale).astype(BF16)
        side()

    n_gc = 2 * d_model // cw
    gate = [None] * n_gc

    def finish_gate(c):
        sg = _sigmoid(gate[c])
        lo_c = c * cw
        if lo_c < d_model:
            ag_ref[0, :, lo_c:lo_c + cw] = (sg * a[c]).astype(BF16)
        else:
            gb_ref[0, :, lo_c - d_model:lo_c - d_model + cw] = sg.astype(BF16)

    for c in range(n_gc):
        gate[c] = _dot(h, win_ref[:, g0 + c * cw:g0 + (c + 1) * cw])
        if c > 0:
            finish_gate(c - 1)
        side()
    finish_gate(n_gc - 1)
    side(len(todo))

    @pl.when(t == nt - 1)
    def _():
        keep = POOL_CTX * rows_per_tok
        pool_ref[0] = ubuf[halo + tm - keep:halo + tm, :]

    if stream is not None:
        @pl.when(step + 2 < n_steps)
        def _():
            for cp in batch_copies(step + 2, side_set):
                cp.start()


def _inproj_call(x, mod, state, wts, *, tm, rows_per_tok, halo, pos0, n_heads, hd, block_means, pages=None):
    n, s, d = x.shape
    nt = s // tm
    d_att = n_heads * hd
    pw = wts["p_scale"].shape[1]
    keep = POOL_CTX * rows_per_tok
    stream = None
    if pages is not None:
        page_table, ck_t = pages
        n_seq, n_pages = page_table.shape
        _, km_rows, page = ck_t.shape
        gpsq = n_pages // _KM_GROUP
        total = n_seq * gpsq
        gps = total // (n * nt)
        assert n_pages % _KM_GROUP == 0 and total == gps * n * nt and gps >= 1 and km_rows % _KM_ROWS == 0
        assert gpsq % gps == 0
        stream = (gps, gpsq, total, MOBA_BLOCK // page)
    body = functools.partial(_inproj_body, tm=tm, rows_per_tok=rows_per_tok, halo=halo, pos0=pos0, nt=nt,
                             n_heads=n_heads, hd=hd, scale=hd ** -0.5 * LOG2E, block_means=block_means, stream=stream)
    rm = mod.shape[2]

    def cs(shape):
        nd = len(shape)
        return pl.BlockSpec(shape, lambda *_: (0,) * nd, pipeline_mode=pl.Buffered(1))

    def bs(shape, fn):
        return pl.BlockSpec(shape, lambda i, t, *_: fn(i, t))

    in_specs = [
        bs((1, tm, d), lambda i, t: (i, t, 0)),
        bs((1, N_MOD, rm, d), lambda i, t: (i, 0, 0, 0)),
        cs((1, d)),
        cs(wts["w_in"].shape),
        cs((1, d_att)), cs((d_att, 1)), cs((MXU_DIM, MXU_DIM)),
        cs(wts["w_pg"].shape), cs((1, pw)), cs(wts["w_bp"].shape),
        bs((1, halo, pw), lambda i, t: (i, 0, 0)),
    ]
    out_specs = [
        bs((1, d_att, tm), lambda i, t: (i, 0, t)),
        bs((1, d_att, tm), lambda i, t: (i, 0, t)),
        bs((1, tm, d_att), lambda i, t: (i, t, 0)),
        bs((1, tm, d_att), lambda i, t: (i, t, 0)),
        bs((1, d_att, LANES), lambda i, t: (i, 0, 0)),
        bs((1, tm, d), lambda i, t: (i, t, 0)),
        bs((1, tm, d), lambda i, t: (i, t, 0)),
        bs((1, keep, pw), lambda i, t: (i, 0, 0)),
    ]
    out_shape = [
        jax.ShapeDtypeStruct((n, d_att, s), F32),
        jax.ShapeDtypeStruct((n, d_att, s), F32),
        jax.ShapeDtypeStruct((n, s, d_att), BF16),
        jax.ShapeDtypeStruct((n, s, d_att), BF16),
        jax.ShapeDtypeStruct((n, d_att, LANES), F32),
        jax.ShapeDtypeStruct((n, s, d), BF16),
        jax.ShapeDtypeStruct((n, s, d), BF16),
        jax.ShapeDtypeStruct((n, keep, pw), F32),
    ]
    n_lvl = max(len(POOL_WINDOWS) - 1, 1)
    scratch = [pltpu.VMEM((halo + tm, pw), F32), pltpu.VMEM((len(POOL_WINDOWS) * n_lvl, halo + tm, LANES), F32),
               pltpu.VMEM((2, d_att, d), BF16)]
    args = [x, mod, wts["g_mix"], wts["w_in"], wts["g_q"], wts["g_k"],
            wts["bd"], wts["w_pg"], wts["p_scale"], wts["w_bp"], state]
    n_prefetch = 0
    if stream is not None:
        gps, gpsq = stream[0], stream[1]
        in_specs.append(pl.BlockSpec(memory_space=pl.ANY))
        out_specs.append(bs((1, km_rows, LANES), lambda i, t: ((i * nt + t) * gps // gpsq, 0, 0)))
        out_shape.append(jax.ShapeDtypeStruct((n_seq, km_rows, LANES), F32))
        scratch += [pltpu.VMEM((2 * gps * _KM_GROUP, km_rows, page), F32), pltpu.SemaphoreType.DMA((2,))]
        args = [page_table] + args + [ck_t]
        n_prefetch = 1
    grid_spec = pltpu.PrefetchScalarGridSpec(num_scalar_prefetch=n_prefetch, grid=(n, nt), in_specs=in_specs,
                                             out_specs=out_specs, scratch_shapes=scratch)
    return pl.pallas_call(
        body, grid_spec=grid_spec, out_shape=out_shape,
        compiler_params=pltpu.CompilerParams(dimension_semantics=("arbitrary", "arbitrary"),
                                             vmem_limit_bytes=VMEM_LIMIT),
        name="mixer_inproj",
    )(*args)


def _attn_prompt_body(q_ref, k2_ref, kmt_ref, vt_ref, o_ref, va_ref, qh_ref, pen_ref, p_ref,
                      *, seq, nb, hd, n_pairs):
    blk = MOBA_BLOCK
    nbp = SUBLANES
    pw = 2 * hd
    lane_q = lax.broadcasted_iota(jnp.int32, (seq, pw), 1)
    lane_m = lax.broadcasted_iota(jnp.int32, (2 * SUBLANES, pw), 1)
    tpos = lax.broadcasted_iota(jnp.int32, (nbp, seq), 1)
    bidx = lax.broadcasted_iota(jnp.int32, (nbp, seq), 0)
    qblk = jnp.zeros((nbp, seq), jnp.int32)
    for b in range(1, nb):
        qblk = qblk + (tpos >= b * blk).astype(jnp.int32)
    ones_rows = (lax.broadcasted_iota(jnp.int32, (SUBLANES, seq), 0) == 0).astype(F32)
    kr = lax.broadcasted_iota(jnp.int32, (blk, blk), 0)
    qc = lax.broadcasted_iota(jnp.int32, (blk, blk), 1)
    causal = kr <= qc

    def setup_pieces(pair):
        ps = slice(pair * pw, (pair + 1) * pw)

        def values_piece(hh):
            rows = slice(pair * pw + hh * hd, pair * pw + (hh + 1) * hd)
            va = jnp.concatenate([vt_ref[0, rows, :], ones_rows, jnp.zeros((SUBLANES, seq), F32)], axis=0)
            va_ref[2 * (pair % 2) + hh] = va.astype(BF16)

        def select_piece(hh):
            q2 = q_ref[0, :, ps]
            km = jnp.transpose(kmt_ref[0, ps, :])
            in_head = (lane_m >= hh * hd) & (lane_m < (hh + 1) * hd)
            kmh = jnp.where(in_head, km[0:2 * SUBLANES], 0.0).astype(BF16)
            sct = _dot_nt(kmh, q2)[0:nbp]
            rank = jnp.zeros((nbp, seq), F32)
            for bp in range(nb):
                rowv = sct[bp:bp + 1, :]
                beats = (rowv > sct) | ((rowv == sct) & (bp < bidx))
                rank = rank + jnp.where(beats & (bp < qblk), 1.0, 0.0)
            keep = (bidx < qblk) & (rank < MOBA_TOPK)
            pen_ref[2 * (pair % 2) + hh] = jnp.where(keep, 0.0, NEG)
            own_lanes = (lane_q >= hh * hd) & (lane_q < (hh + 1) * hd)
            qh_ref[2 * (pair % 2) + hh] = jnp.where(own_lanes, q2, jnp.zeros_like(q2))

        return [functools.partial(values_piece, 0), functools.partial(values_piece, 1),
                functools.partial(select_piece, 0), functools.partial(select_piece, 1)]

    def main(pair, fillers):
        ps = slice(pair * pw, (pair + 1) * pw)
        items = [(i, hh) for i in range(nb) for hh in range(2)]
        m_of, zero_of, o_of, s_of = {}, {}, {}, {}

        def base_of(i):
            return blk * (i * (i + 1) // 2)

        def logits_piece(k, j):
            i, hh = items[k]
            qs = slice(i * blk, (i + 1) * blk)
            sv = _dot_nt(k2_ref[0, j * blk:(j + 1) * blk, ps], qh_ref[2 * (pair % 2) + hh, qs, :])
            s_of[k, j] = sv
            if j == i:
                mj = jnp.max(jnp.where(causal, sv, NEG), axis=0, keepdims=True)
            else:
                mj = jnp.max(sv, axis=0, keepdims=True) + pen_ref[2 * (pair % 2) + hh, j:j + 1, qs]
            m_of[k] = mj if j == 0 else jnp.maximum(m_of[k], mj)

        def exp_piece(k, j):
            i, hh = items[k]
            qs = slice(i * blk, (i + 1) * blk)
            if j == 0:
                zero_of[k] = [jnp.zeros((1, blk), F32)] * _EXP_DEPTH
            shift = m_of[k] if j == i else m_of[k] - pen_ref[2 * (pair % 2) + hh, j:j + 1, qs]
            sv = s_of.pop((k, j))
            for c in range(blk // _EXP_ROWS):
                r0 = base_of(i) + j * blk + c * _EXP_ROWS
                x = sv[c * _EXP_ROWS:(c + 1) * _EXP_ROWS]
                if j == i:
                    x = jnp.where(causal[c * _EXP_ROWS:(c + 1) * _EXP_ROWS], x, NEG)
                p = jnp.exp2(x - (shift + zero_of[k][c % _EXP_DEPTH]))
                p_ref[2 * (pair % 2) + hh, r0:r0 + _EXP_ROWS, :] = p.astype(BF16)
                zero_of[k][c % _EXP_DEPTH] = jnp.minimum(p[_EXP_ROWS - 1:_EXP_ROWS], 0.0)

        def values(k):
            i, hh = items[k]
            ln = (i + 1) * blk
            ot = _dot(va_ref[2 * (pair % 2) + hh, :, 0:ln], p_ref[2 * (pair % 2) + hh, base_of(i):base_of(i) + ln, :])
            o_of[k] = ot[0:hd] / ot[hd:hd + 1]
            if hh == 1:
                both = jnp.concatenate([o_of.pop(k - 1), o_of.pop(k)], axis=0)
                o_ref[0, i * blk:(i + 1) * blk, ps] = jnp.transpose(both).astype(BF16)

        for j in range(items[0][0] + 1):
            logits_piece(0, j)
        for k in range(len(items)):
            ahead = [(k + 1, j) for j in range(items[k + 1][0] + 1)] if k + 1 < len(items) else []
            for j in range(items[k][0] + 1):
                exp_piece(k, j)
                if ahead:
                    logits_piece(*ahead.pop(0))
            for kj in ahead:
                logits_piece(*kj)
            values(k)
            if fillers and k >= 2 and k % 2 == 0:
                fillers.pop(0)()
        for f in fillers:
            f()

    for piece in setup_pieces(0):
        piece()
    for pair in range(n_pairs):
        main(pair, setup_pieces(pair + 1) if pair + 1 < n_pairs else [])


def _attn_prompt_call(q, k2, kmt, vt, *, n_heads, hd):
    n, s, d_att = q.shape
    nb = s // MOBA_BLOCK
    n_pairs = n_heads // 2
    n_slot = 2 * min(n_pairs, 2)
    wd = n_pairs * 2 * hd
    body = functools.partial(_attn_prompt_body, seq=s, nb=nb, hd=hd, n_pairs=n_pairs)
    tri = MOBA_BLOCK * (nb * (nb + 1) // 2)
    tok_spec = pl.BlockSpec((1, s, wd), lambda i, p: (i, 0, p))
    return pl.pallas_call(
        body, grid=(n, d_att // wd),
        in_specs=[tok_spec, tok_spec,
                  pl.BlockSpec((1, wd, LANES), lambda i, p: (i, p, 0)),
                  pl.BlockSpec((1, wd, s), lambda i, p: (i, p, 0))],
        out_specs=tok_spec,
        out_shape=jax.ShapeDtypeStruct((n, s, d_att), BF16),
        scratch_shapes=[pltpu.VMEM((n_slot, hd + 2 * SUBLANES, s), BF16),
                        pltpu.VMEM((n_slot, s, 2 * hd), BF16), pltpu.VMEM((n_slot, SUBLANES, s), F32),
                        pltpu.VMEM((n_slot, tri, MOBA_BLOCK), BF16)],
        compiler_params=pltpu.CompilerParams(dimension_semantics=("arbitrary", "arbitrary"),
                                             vmem_limit_bytes=VMEM_LIMIT),
        name="moba_prompt",
    )(q, k2, kmt, vt)


def _post_body(x_ref, o_ref, ag_ref, gb_ref, mod_ref, wba_ref, wout_ref, gffn_ref, wup_ref, wconv_ref, bconv_ref,
               wdown_ref, cst_ref, y_ref, conv_ref, gbuf, act_ref, *, tm, rows_per_tok, halo, nt, pc, down_groups, d_ff):
    t = pl.program_id(1)
    x = x_ref[0]
    gt_m = mod_ref[0, 2]
    sh_f = mod_ref[0, 3]
    sc_f = mod_ref[0, 4]
    gt_f = mod_ref[0, 5]
    oa = _dot(o_ref[0], wba_ref[...])
    merged = ag_ref[0].astype(F32) + gb_ref[0].astype(F32) * oa
    x1 = x + gt_m * _dot(merged.astype(BF16), wout_ref[...])
    ms = jnp.mean(x1 * x1, axis=-1, keepdims=True)
    h2 = x1 * lax.rsqrt(ms + EPS) * gffn_ref[...]
    h2 = (h2 * (1.0 + sc_f) + sh_f).astype(BF16)

    @pl.when(t == 0)
    def _():
        gbuf[0:halo, :] = cst_ref[0]

    @pl.when(t > 0)
    def _():
        gbuf[0:halo, :] = gbuf[tm:tm + halo, :]

    r = rows_per_tok
    n_pc = d_ff // pc

    def up(c):
        cs = slice(c * pc, (c + 1) * pc)
        fg = _dot(h2, wup_ref[:, cs])
        fv = _dot(h2, wup_ref[:, d_ff + c * pc:d_ff + (c + 1) * pc])
        gbuf[halo:halo + tm, cs] = fg
        return fg, fv

    def activate(c, fg, fv):
        cs = slice(c * pc, (c + 1) * pc)
        conv = bconv_ref[:, cs]
        conv = conv + gbuf[halo - 2 * r:halo - 2 * r + tm, cs] * wconv_ref[0:1, cs]
        conv = conv + gbuf[halo - r:halo - r + tm, cs] * wconv_ref[1:2, cs]
        conv = conv + fg * wconv_ref[2:3, cs]
        act_ref[:, cs] = (conv * _sigmoid(conv) * fv).astype(BF16)

    bounds = [round(g * n_pc / down_groups) * pc for g in range(down_groups + 1)]
    acc = None
    nxt = up(0)
    for c in range(n_pc):
        cur = nxt
        if c + 1 < n_pc:
            nxt = up(c + 1)
        activate(c, *cur)
        if (c + 1) * pc in bounds[1:]:
            lo = bounds[bounds.index((c + 1) * pc) - 1]
            part = _dot(act_ref[:, lo:(c + 1) * pc], wdown_ref[lo:(c + 1) * pc, :])
            acc = part if acc is None else acc + part
    y_ref[0] = x1 + gt_f * acc

    @pl.when(t == nt - 1)
    def _():
        keep = (CONV_W - 1) * r
        conv_ref[0] = gbuf[halo + tm - keep:halo + tm, :]


def _post_call(x, o, ag, gb, mod, cstate, wts, *, tm, rows_per_tok, halo, pc, down_groups):
    n, s, d = x.shape
    nt = s // tm
    d_att = o.shape[-1]
    d_ff = wts["w_down"].shape[0]
    keep = (CONV_W - 1) * rows_per_tok
    rm = mod.shape[2]
    body = functools.partial(_post_body, tm=tm, rows_per_tok=rows_per_tok, halo=halo, nt=nt, pc=pc,
                             down_groups=down_groups, d_ff=d_ff)
    row_spec = lambda w: pl.BlockSpec((1, tm, w), lambda i, t: (i, t, 0))
    in_specs = [
        row_spec(d), row_spec(d_att), row_spec(d), row_spec(d),
        pl.BlockSpec((1, N_MOD, rm, d), lambda i, t: (i, 0, 0, 0)),
        _const_spec(wts["w_ba"].shape), _const_spec(wts["w_out"].shape), _const_spec((1, d)),
        _const_spec(wts["w_up"].shape), _const_spec((CONV_W, d_ff)), _const_spec((1, d_ff)),
        _const_spec(wts["w_down"].shape),
        pl.BlockSpec((1, halo, d_ff), lambda i, t: (i, 0, 0)),
    ]
    out_specs = [row_spec(d), pl.BlockSpec((1, keep, d_ff), lambda i, t: (i, 0, 0))]
    out_shape = [jax.ShapeDtypeStruct((n, s, d), F32), jax.ShapeDtypeStruct((n, keep, d_ff), F32)]
    return pl.pallas_call(
        body, grid=(n, nt), in_specs=in_specs, out_specs=out_specs, out_shape=out_shape,
        scratch_shapes=[pltpu.VMEM((halo + tm, d_ff), F32), pltpu.VMEM((tm, d_ff), BF16)],
        compiler_params=pltpu.CompilerParams(dimension_semantics=("arbitrary", "arbitrary"),
                                             vmem_limit_bytes=VMEM_LIMIT),
        name="merge_ffn",
    )(x, o, ag, gb, mod, wts["w_ba"], wts["w_out"], wts["g_ffn"], wts["w_up"], wts["w_conv"], wts["b_conv"],
      wts["w_down"], cstate)


def _kmean_body(pt_ref, ck_ref, km_ref, buf, sem, *, n_seq, n_pages, ppb):
    n = pl.program_id(0)
    rows = km_ref.shape[1]
    bpg = _KM_GROUP // ppb
    n_groups = n_pages // _KM_GROUP

    def copy(seq, p, slot):
        return pltpu.make_async_copy(ck_ref.at[pt_ref[seq, p]], buf.at[slot], sem.at[slot])

    def start_group(seq, g, half):
        for k in range(_KM_GROUP):
            copy(seq, g * _KM_GROUP + k, half * _KM_GROUP + k).start()

    @pl.when(n == 0)
    def _():
        start_group(0, 0, 0)
        start_group(0, 1, 1)

    km_ref[0] = jnp.zeros((rows, LANES), F32)
    lane = lax.broadcasted_iota(jnp.int32, (_KM_ROWS, LANES), 1)

    def pair(g2, carry):
        for half in range(2):
            g = g2 * 2 + half
            for k in range(_KM_GROUP):
                copy(n, g * _KM_GROUP + k, half * _KM_GROUP + k).wait()
            for rc in range(rows // _KM_ROWS):
                rs = slice(rc * _KM_ROWS, (rc + 1) * _KM_ROWS)
                upd = jnp.zeros((_KM_ROWS, LANES), F32)
                for bb in range(bpg):
                    slot = half * _KM_GROUP + bb * ppb
                    x = buf[slot, rs, :]
                    for pp in range(1, ppb):
                        x = x + buf[slot + pp, rs, :]
                    col = jnp.sum(x, axis=-1, keepdims=True)
                    upd = jnp.where(lane == g * bpg + bb, col, upd)
                km_ref[0, rs, :] = km_ref[0, rs, :] + upd * (1.0 / MOBA_BLOCK)
            nxt = g + 2

            @pl.when(nxt < n_groups)
            def _():
                start_group(n, nxt, half)

            @pl.when((nxt >= n_groups) & (n + 1 < n_seq))
            def _():
                start_group(n + 1, nxt - n_groups, half)
        return carry

    lax.fori_loop(0, n_groups // 2, pair, 0)


def _kmean_call(page_table, ck_t):
    n, n_pages = page_table.shape
    _, rows, page = ck_t.shape
    ppb = MOBA_BLOCK // page
    assert n_pages % (2 * _KM_GROUP) == 0 and rows % _KM_ROWS == 0
    body = functools.partial(_kmean_body, n_seq=n, n_pages=n_pages, ppb=ppb)
    grid_spec = pltpu.PrefetchScalarGridSpec(
        num_scalar_prefetch=1, grid=(n,),
        in_specs=[pl.BlockSpec(memory_space=pl.ANY)],
        out_specs=pl.BlockSpec((1, rows, LANES), lambda i, pt: (i, 0, 0)),
        scratch_shapes=[pltpu.VMEM((2 * _KM_GROUP, rows, page), F32), pltpu.SemaphoreType.DMA((2 * _KM_GROUP,))],
    )
    return pl.pallas_call(
        body, grid_spec=grid_spec,
        out_shape=jax.ShapeDtypeStruct((n, rows, LANES), F32),
        compiler_params=pltpu.CompilerParams(dimension_semantics=("arbitrary",), vmem_limit_bytes=VMEM_LIMIT),
        name="cache_block_means",
    )(page_table, ck_t)


def _topk_body(q_ref, km_ref, sel_ref, *, n_heads, hd, n_blk, top, unroll):
    n_seq = q_ref.shape[0]
    rows = n_heads * SUBLANES
    lane_q = lax.broadcasted_iota(jnp.int32, (SUBLANES, n_heads * hd), 1)
    lane = lax.broadcasted_iota(jnp.int32, (unroll * rows, LANES), 1).astype(F32)

    def per_group(ig, carry):
        scs = []
        for k in range(unroll):
            i = ig * unroll + k
            q8 = q_ref[i]
            qbd = jnp.concatenate(
                [jnp.where((lane_q >= hh * hd) & (lane_q < (hh + 1) * hd), q8, 0.0) for hh in range(n_heads)], axis=0)
            scs.append(_dot(qbd.astype(BF16), km_ref[i].astype(BF16)))
        sc = jnp.where(lane < n_blk, jnp.concatenate(scs, axis=0), -jnp.inf)
        out = jnp.zeros((unroll * rows, LANES), F32)
        for r in range(top):
            m = jnp.max(sc, axis=-1, keepdims=True)
            idx = jnp.min(jnp.where(sc == m, lane, float(LANES)), axis=-1, keepdims=True)
            out = jnp.where(lane == r, idx, out)
            sc = jnp.where(lane == idx, -jnp.inf, sc)
        out = out.astype(jnp.int32)
        for k in range(unroll):
            sel_ref[ig * unroll + k] = out[k * rows:(k + 1) * rows]
        return carry

    lax.fori_loop(0, n_seq // unroll, per_group, 0)


def _topk_call(q8, kmt, *, n_heads, hd, n_blk, top):
    n = q8.shape[0]
    rows = n_heads * SUBLANES
    unroll = max(u for u in (8, 4, 2, 1) if n % u == 0)
    body = functools.partial(_topk_body, n_heads=n_heads, hd=hd, n_blk=n_blk, top=top, unroll=unroll)
    return pl.pallas_call(
        body, grid=(1,),
        in_specs=[pl.BlockSpec(q8.shape, lambda i: (0, 0, 0)), pl.BlockSpec(kmt.shape, lambda i: (0, 0, 0))],
        out_specs=pl.BlockSpec((n, rows, LANES), lambda i: (0, 0, 0)),
        out_shape=jax.ShapeDtypeStruct((n, rows, LANES), jnp.int32),
        compiler_params=pltpu.CompilerParams(vmem_limit_bytes=VMEM_LIMIT),
        name="block_topk",
    )(q8, kmt)


def _attn_sample_body(pt_ref, sel_ref, qt_ref, knt_ref, vnt_ref, lseq_ref, ltok_ref, ck_ref, cv_ref, o_ref, kbuf, vbuf,
                      lg_ref, p_ref, sem,
                      *, n_seq, n_heads, hd, n_tok, top, ppb):
    n = pl.program_id(0)
    n_sel = top * ppb
    n_grp = n_heads * n_tok
    slot = lax.rem(n, 2)

    def gather(seq, sl):
        for hh in range(n_heads):
            for s in range(n_tok):
                g = hh * n_tok + s
                for j in range(n_sel):
                    r, pp = divmod(j, ppb)
                    page = pt_ref[seq, sel_ref[seq, g * top + r] * ppb + pp]
                    pltpu.make_async_copy(ck_ref.at[page, hh], kbuf.at[sl, g, j], sem.at[sl]).start()
                    pltpu.make_async_copy(cv_ref.at[page, hh], vbuf.at[sl, g, j], sem.at[sl]).start()

    @pl.when(n == 0)
    def _():
        gather(0, 0)

    @pl.when(n + 1 < n_seq)
    def _():
        gather(n + 1, 1 - slot)

    for g in range(n_grp):
        for j in range(n_sel):
            pltpu.make_async_copy(ck_ref.at[0, 0], kbuf.at[slot, g, j], sem.at[slot]).wait()
            pltpu.make_async_copy(cv_ref.at[0, 0], vbuf.at[slot, g, j], sem.at[slot]).wait()

    mine = lseq_ref[...] == n
    lane_tok = ltok_ref[...]
    pad_rows = [jnp.full((1, LANES), NEG, F32)] * (SUBLANES - n_sel - 1)
    for hh in range(n_heads):
        rs = slice(hh * hd, (hh + 1) * hd)
        knt = knt_ref[rs, :]
        for s in range(n_tok):
            g = hh * n_tok + s
            qb = jnp.broadcast_to(qt_ref[0, rs, s:s + 1], (hd, LANES))
            rows = [jnp.sum(kbuf[slot, g, j] * qb, axis=0, keepdims=True) for j in range(n_sel)]
            own = jnp.sum(knt * qb, axis=0, keepdims=True)
            rows.append(jnp.where(mine & (lane_tok <= s), own, NEG))
            lg_ref[g] = jnp.concatenate(rows + pad_rows, axis=0)

    lg = lg_ref[...]
    m = jnp.max(jnp.max(lg, axis=1), axis=-1, keepdims=True)
    mb = jnp.broadcast_to(m, (n_grp, LANES))
    for g in range(n_grp):
        p_ref[g] = jnp.exp2(lg_ref[g] - mb[g:g + 1, :])
    l = jnp.sum(jnp.sum(p_ref[...], axis=1), axis=-1, keepdims=True)
    inv = jnp.broadcast_to(1.0 / l, (n_grp, LANES))

    lane_o = lax.broadcasted_iota(jnp.int32, (hd, LANES), 1)
    outs = []
    for hh in range(n_heads):
        rs = slice(hh * hd, (hh + 1) * hd)
        vnt = vnt_ref[rs, :]
        out_h = jnp.zeros((hd, LANES), F32)
        for s in range(n_tok):
            g = hh * n_tok + s
            pg = p_ref[g] * inv[g:g + 1, :]
            acc = vnt * pg[n_sel:n_sel + 1, :]
            for j in range(n_sel):
                acc = acc + vbuf[slot, g, j] * pg[j:j + 1, :]
            col = jnp.sum(acc, axis=-1, keepdims=True)
            out_h = jnp.where(lane_o == s, col, out_h)
        outs.append(out_h)
    o_ref[0] = jnp.transpose(jnp.concatenate(outs, axis=0))[0:SUBLANES]


def _attn_sample_call(page_table, sel, qt, knt, vnt, lane_seq, lane_tok, ck_t, cv_t, *, n_heads, hd, n_tok, top):
    n = page_table.shape[0]
    page = ck_t.shape[-1]
    ppb = MOBA_BLOCK // page
    d_att = n_heads * hd
    n_grp = n_heads * n_tok
    assert top * ppb + 1 <= SUBLANES
    body = functools.partial(_attn_sample_body, n_seq=n, n_heads=n_heads, hd=hd, n_tok=n_tok, top=top, ppb=ppb)
    blk = pl.BlockSpec((1, d_att, LANES), lambda i, pt, sl: (i, 0, 0))
    whole = lambda shape: pl.BlockSpec(shape, lambda i, pt, sl: (0,) * len(shape))
    grid_spec = pltpu.PrefetchScalarGridSpec(
        num_scalar_prefetch=2, grid=(n,),
        in_specs=[blk, whole((d_att, LANES)), whole((d_att, LANES)), whole((1, LANES)), whole((1, LANES)),
                  pl.BlockSpec(memory_space=pl.ANY), pl.BlockSpec(memory_space=pl.ANY)],
        out_specs=pl.BlockSpec((1, SUBLANES, d_att), lambda i, pt, sl: (i, 0, 0)),
        scratch_shapes=[pltpu.VMEM((2, n_grp, top * ppb, hd, page), F32),
                        pltpu.VMEM((2, n_grp, top * ppb, hd, page), F32),
                        pltpu.VMEM((n_grp, SUBLANES, LANES), F32),
                        pltpu.VMEM((n_grp, SUBLANES, LANES), F32),
                        pltpu.SemaphoreType.DMA((2,))],
    )
    return pl.pallas_call(
        body, grid_spec=grid_spec,
        out_shape=jax.ShapeDtypeStruct((n, SUBLANES, d_att), F32),
        compiler_params=pltpu.CompilerParams(dimension_semantics=("arbitrary",), vmem_limit_bytes=VMEM_LIMIT),
        name="moba_sample",
    )(page_table, sel, qt, knt, vnt, lane_seq, lane_tok, ck_t, cv_t)


def _layer_weights(l, g_norm_mix, w_in, g_q, g_k, w_pool_group, pool_scale, w_branch_pool, w_branch_attn, w_out,
                   g_norm_ffn, w_up, w_conv, b_conv, w_down, n_heads, hd):
    d = w_in.shape[1]
    pw = w_pool_group.shape[1] * w_pool_group.shape[2]
    d_att = n_heads * hd
    wi = w_in[l].astype(BF16)
    blk = jnp.arange(MXU_DIM) // hd
    return {
        "g_mix": g_norm_mix[l].reshape(1, d),
        "w_in": wi,
        "g_q": g_q[l].reshape(1, d_att), "g_k": g_k[l].reshape(d_att, 1),
        "bd": jnp.where(blk[:, None] == blk[None, :], 1.0 / hd, 0.0).astype(BF16),
        "w_pg": w_pool_group[l].astype(BF16), "p_scale": pool_scale[l].reshape(1, pw),
        "w_bp": w_branch_pool[l].astype(BF16),
        "w_ba": w_branch_attn[l].astype(BF16), "w_out": w_out[l].astype(BF16),
        "g_ffn": g_norm_ffn[l].reshape(1, d), "w_up": w_up[l].astype(BF16),
        "w_conv": w_conv[l], "b_conv": b_conv[l].reshape(1, -1), "w_down": w_down[l].astype(BF16),
    }


def kernel(x_prompt, x_sample, cache_k, cache_v, state_pool, state_ffn_conv, page_table, c_prompt, c_sample, w_ada, b_ada, g_norm_mix, w_in, g_q, g_k, w_pool_group, pool_scale, w_branch_pool, w_branch_attn, w_out, g_norm_ffn, w_up, w_conv, b_conv, w_down):
    nb_p, seq, d = x_prompt.shape
    nb_s, n_tok, _ = x_sample.shape
    depth, n_phys, page, n_heads, hd = cache_k.shape
    d_att = n_heads * hd
    pw = state_pool.shape[-1]
    d_ff = w_down.shape[1]
    n_pages = page_table.shape[1]
    past = n_pages * page
    assert MOBA_BLOCK % page == 0 and past % MOBA_BLOCK == 0 and seq % MOBA_BLOCK == 0
    assert n_tok <= SUBLANES and nb_s * n_tok <= LANES
    n_blk = past // MOBA_BLOCK
    top = min(MOBA_TOPK, n_blk)
    assert 0 < n_blk <= LANES and seq // MOBA_BLOCK <= SUBLANES
    tm = 256
    tm_ffn = 512 if seq % 512 == 0 else tm
    pc, dg = MXU_DIM, 3
    assert d_ff % pc == 0
    rows_s = nb_s * n_tok

    y_p, y_s = x_prompt, x_sample.transpose(1, 0, 2).reshape(1, rows_s, d)
    halo_p = 4 * SUBLANES
    zero_pool = jnp.zeros((nb_p, halo_p, pw), F32)
    zero_conv = jnp.zeros((nb_p, SUBLANES, d_ff), F32)
    outs = [[] for _ in range(8)]
    for l in range(depth):
        wts = _layer_weights(l, g_norm_mix, w_in, g_q, g_k, w_pool_group, pool_scale, w_branch_pool, w_branch_attn,
                             w_out, g_norm_ffn, w_up, w_conv, b_conv, w_down, n_heads, hd)
        mod = _ada_call(jnp.concatenate([c_prompt, c_sample], axis=0), w_ada[l], b_ada[l])
        mod_p = mod[:nb_p].reshape(nb_p, N_MOD, 1, d)
        mod_s = jnp.tile(mod[nb_p:].reshape(nb_s, N_MOD, d), (n_tok, 1, 1)).transpose(1, 0, 2)[None]

        ck_t = cache_k[l].transpose(0, 2, 3, 1)
        cv_t = cache_v[l].transpose(0, 2, 3, 1)
        ck_flat = ck_t.reshape(n_phys, d_att, page)
        gpsq, steps = n_pages // _KM_GROUP, nb_p * (seq // tm)
        gps = nb_s * gpsq // steps
        fuse = n_pages % _KM_GROUP == 0 and gps >= 1 and gps * steps == nb_s * gpsq and gpsq % gps == 0
        res = _inproj_call(y_p, mod_p, zero_pool, wts, tm=tm, rows_per_tok=1, halo=halo_p, pos0=0,
                           n_heads=n_heads, hd=hd, block_means=True, pages=(page_table, ck_flat) if fuse else None)
        kt, vt, q, k2, kmt, ag, gb, pool_p = res[:8]
        kmt_s = res[8] if fuse else _kmean_call(page_table, ck_flat)
        o = _attn_prompt_call(q, k2, kmt, vt, n_heads=n_heads, hd=hd)
        y_p, conv_p = _post_call(y_p, o, ag, gb, mod_p, zero_conv, wts, tm=tm_ffn, rows_per_tok=1, halo=SUBLANES, pc=pc,
                                 down_groups=dg)
        to_rows = lambda a: a.reshape(nb_p, n_heads, hd, seq).transpose(0, 3, 1, 2)
        outs[0].append(to_rows(kt)); outs[1].append(to_rows(vt)); outs[2].append(pool_p); outs[3].append(conv_p)

        pool_state = state_pool[l].transpose(1, 0, 2).reshape(1, POOL_CTX * nb_s, pw)
        conv_state = state_ffn_conv[l].transpose(1, 0, 2).reshape(1, (CONV_W - 1) * nb_s, d_ff)
        kt_s, vt_s, q_s, _, _, ag_s, gb_s, pool_s = _inproj_call(
            y_s, mod_s, pool_state, wts, tm=rows_s, rows_per_tok=nb_s, halo=POOL_CTX * nb_s, pos0=past,
            n_heads=n_heads, hd=hd, block_means=False)
        q_ns = q_s.reshape(n_tok, nb_s, d_att).transpose(1, 0, 2).astype(F32)
        q8 = jnp.pad(q_ns, ((0, 0), (0, SUBLANES - n_tok), (0, 0)))
        sel = _topk_call(q8, kmt_s, n_heads=n_heads, hd=hd, n_blk=n_blk, top=top)
        sel = sel.reshape(nb_s, n_heads, SUBLANES, LANES)[:, :, :n_tok, :top].reshape(nb_s, n_heads * n_tok * top)
        qt = jnp.pad(q_ns.transpose(0, 2, 1), ((0, 0), (0, 0), (0, LANES - n_tok)))
        lane_pad = lambda a: jnp.pad(a, ((0, 0), (0, LANES - rows_s)))
        lanes = jnp.arange(LANES)
        lane_seq = jnp.where(lanes < rows_s, lanes % nb_s, -1).astype(jnp.int32).reshape(1, LANES)
        lane_tok = (lanes // nb_s).astype(jnp.int32).reshape(1, LANES)
        ot = _attn_sample_call(page_table, sel, qt, lane_pad(kt_s[0]), lane_pad(vt_s[0]), lane_seq, lane_tok, ck_t, cv_t,
                               n_heads=n_heads, hd=hd, n_tok=n_tok, top=top)
        o_s = ot[:, :n_tok].transpose(1, 0, 2).reshape(1, rows_s, d_att).astype(BF16)
        y_s, conv_s = _post_call(y_s, o_s, ag_s, gb_s, mod_s, conv_state, wts, tm=rows_s, rows_per_tok=nb_s,
                                 halo=(CONV_W - 1) * nb_s, pc=pc, down_groups=dg)
        to_rows_s = lambda a: a[0].reshape(n_heads, hd, n_tok, nb_s).transpose(3, 2, 0, 1)
        outs[4].append(to_rows_s(kt_s)); outs[5].append(to_rows_s(vt_s))
        outs[6].append(pool_s.reshape(POOL_CTX, nb_s, pw).transpose(1, 0, 2))
        outs[7].append(conv_s.reshape(CONV_W - 1, nb_s, d_ff).transpose(1, 0, 2))

    y_s_out = y_s.reshape(n_tok, nb_s, d).transpose(1, 0, 2)
    st = [jnp.stack(v) for v in outs]
    return (y_p, y_s_out, st[0], st[1], st[2], st[3], st[4], st[5], st[6], st[7])
```
